```python
import jax, jax.numpy as jnp
from jax import lax
import numpy as np

D_MODEL = 1024
BATCH = 1
SEQ = 16384
DEPTH = 2

GRID_W = 64
CTX_LEN = 256
CHUNK = 128
Q_BLOCK = 128
EPS = 1e-6
A_GROUPS = 4
A_WIDTH = D_MODEL // 4
B_BLOCKS = 4
B_WIDTH = D_MODEL // 4
CONV_W = 4
LRU_C = 8.0
C_NOPE = 128
C_ROPE = 64
C_V = 128
C_HEADS = (D_MODEL // 2) // C_V
C_WIDTH = C_HEADS * C_V
Q_LORA = 384
KV_LORA = 256
ROPE_BASE = 10000.0
ATTN_SCALE = (C_NOPE + C_ROPE) ** -0.5
MIX_WIDTH = A_WIDTH + B_WIDTH + C_WIDTH
IN_SPLITS = (2 * A_WIDTH, 2 * A_WIDTH + B_WIDTH, 2 * A_WIDTH + 2 * B_WIDTH, 2 * A_WIDTH + 2 * B_WIDTH + Q_LORA, 2 * A_WIDTH + 2 * B_WIDTH + Q_LORA + KV_LORA)
IN_COLS = IN_SPLITS[-1] + C_ROPE
N_EXPERTS = 16
EXPERT_FF = 1024
EC_CAPACITY = 2

kernel_name = 'hybrid_gmlp_rglru_mla_ecmoe_dit'


def rmsnorm(x, g):
    xf = x.astype(jnp.float32)
    y = xf * lax.rsqrt(jnp.mean(xf * xf, axis=-1, keepdims=True) + EPS)
    return (y * g.astype(jnp.float32)).astype(x.dtype)


def modulate(h, shift, scale):
    return h * (1 + scale) + shift


def rope_tables(n):
    rows = n // GRID_W
    r = jnp.repeat(jnp.arange(rows), GRID_W).astype(jnp.float32)
    col = jnp.tile(jnp.arange(GRID_W), rows).astype(jnp.float32)
    nf = C_ROPE // 4
    inv = ROPE_BASE ** (-jnp.arange(nf, dtype=jnp.float32) / nf)
    ang = jnp.concatenate([r[:, None] * inv, col[:, None] * inv], axis=-1)
    return jnp.cos(ang), jnp.sin(ang)


def apply_rope(x, cos, sin):
    shp = x.shape
    nf = C_ROPE // 4
    xf = x.astype(jnp.float32).reshape(*shp[:-1], 2, 2, nf)
    cs = cos.reshape(cos.shape[0], 1, 2, nf)
    sn = sin.reshape(sin.shape[0], 1, 2, nf)
    x1 = xf[..., 0, :]
    x2 = xf[..., 1, :]
    out = jnp.stack([x1 * cs - x2 * sn, x2 * cs + x1 * sn], axis=-2).reshape(shp)
    return out.astype(x.dtype)


def chunk_sgu(z, w_s, b_s):
    u, v = jnp.split(z, 2, axis=-1)
    vf = v.astype(jnp.float32)
    v = (vf * lax.rsqrt(jnp.mean(vf * vf, axis=-1, keepdims=True) + EPS)).astype(z.dtype)
    b, t, _ = v.shape
    vb = v.reshape(b, t // CHUNK, CHUNK, A_GROUPS, A_WIDTH // A_GROUPS)
    mixed = jnp.einsum('gpq,bnqgc->bnpgc', w_s, vb) + b_s.T[:, :, None]
    return u * mixed.reshape(b, t, A_WIDTH)


def dwconv(x, w, b):
    t = x.shape[1]
    left = CONV_W // 2
    xp = jnp.pad(x, ((0, 0), (left, CONV_W - 1 - left), (0, 0)))
    y = b
    for k in range(CONV_W):
        y = y + xp[:, k:k + t] * w[k]
    return y


def blockdiag(x, w, b):
    xb = x.reshape(*x.shape[:-1], B_BLOCKS, B_WIDTH // B_BLOCKS)
    return jnp.einsum('btnc,ncd->btnd', xb, w).reshape(x.shape) + b


def lru_coeffs(xc, wa, ba, wx, bx, lam):
    r = jax.nn.sigmoid(blockdiag(xc, wa, ba)).astype(jnp.float32)
    i = jax.nn.sigmoid(blockdiag(xc, wx, bx)).astype(jnp.float32)
    log_a = -LRU_C * r * jax.nn.softplus(-lam.astype(jnp.float32))
    a = jnp.exp(log_a)
    mult = jnp.sqrt(jnp.maximum(-jnp.expm1(2 * log_a), 0.0))
    return a, mult * (i * xc.astype(jnp.float32))


def linear_scan(a, b, reverse):
    def comb(e1, e2):
        a1, b1 = e1
        a2, b2 = e2
        return a1 * a2, a2 * b1 + b2
    return lax.associative_scan(comb, (a, b), reverse=reverse, axis=1)


def rglru_mixer(xb_l, gb_l, xb_c, gb_c, lp, need_ctx_out):
    cl = dwconv(xb_l, lp['conv_w'], lp['conv_b'])
    cc = dwconv(xb_c, lp['conv_w'], lp['conv_b'])
    hl_dirs = []
    hc_dirs = []
    for d in range(2):
        rev = d == 1
        a_c, b_c = lru_coeffs(cc, lp['lru_wa'][d], lp['lru_ba'][d], lp['lru_wx'][d], lp['lru_bx'][d], lp['lru_lambda'][d])
        _, hc = linear_scan(a_c, b_c, rev)
        h0 = hc[:, 0] if rev else hc[:, -1]
        a_l, b_l = lru_coeffs(cl, lp['lru_wa'][d], lp['lru_ba'][d], lp['lru_wx'][d], lp['lru_bx'][d], lp['lru_lambda'][d])
        acum, bcum = linear_scan(a_l, b_l, rev)
        hl_dirs.append(acum * h0[:, None] + bcum)
        hc_dirs.append(hc)
    yl = (hl_dirs[0] + hl_dirs[1]).astype(xb_l.dtype) * jax.nn.gelu(gb_l)
    yc = (hc_dirs[0] + hc_dirs[1]).astype(xb_c.dtype) * jax.nn.gelu(gb_c) if need_ctx_out else None
    return yl, yc


def mla_q(cq, g, w_uq, rope):
    b, t, _ = cq.shape
    q = (rmsnorm(cq, g) @ w_uq).reshape(b, t, C_HEADS, C_NOPE + C_ROPE)
    if rope is not None:
        q = jnp.concatenate([q[..., :C_NOPE], apply_rope(q[..., C_NOPE:], rope[0], rope[1])], axis=-1)
    return q


def mla_kv(ckv, kr, g, w_ukv, rope):
    b, t, _ = ckv.shape
    kv = (rmsnorm(ckv, g) @ w_ukv).reshape(b, t, C_HEADS, C_NOPE + C_V)
    k_nope, v = kv[..., :C_NOPE], kv[..., C_NOPE:]
    kr = kr[:, :, None, :]
    if rope is not None:
        kr = apply_rope(kr, rope[0], rope[1])
    k = jnp.concatenate([k_nope, jnp.broadcast_to(kr, (b, t, C_HEADS, C_ROPE))], axis=-1)
    return k, v


def attend(q, k, v):
    s = jnp.einsum('bqhd,bkhd->bhqk', q, k, preferred_element_type=jnp.float32) * ATTN_SCALE
    p = jax.nn.softmax(s, axis=-1).astype(v.dtype)
    return jnp.einsum('bhqk,bkhd->bqhd', p, v)


def latent_attention(q, k, v):
    b, t, h, dk = q.shape
    qb = q.reshape(b, t // Q_BLOCK, Q_BLOCK, h, dk).swapaxes(0, 1)
    ob = lax.map(lambda qq: attend(qq, k, v), qb)
    return ob.swapaxes(0, 1).reshape(b, t, h * v.shape[-1])


def ec_moe(h, w_router, w_gate, w_up, w_down):
    b, t, d = h.shape
    cap = EC_CAPACITY * t // N_EXPERTS
    logits = jnp.einsum('btd,de->bte', h, w_router, preferred_element_type=jnp.float32)
    probs = jax.nn.softmax(logits, axis=-1)
    g, idx = lax.top_k(jnp.swapaxes(probs, 1, 2), cap)
    xs = jax.vmap(lambda hb, ib: hb[ib])(h, idx)
    hid = jax.nn.silu(jnp.einsum('becd,edf->becf', xs, w_gate)) * jnp.einsum('becd,edf->becf', xs, w_up)
    ye = jnp.einsum('becf,efd->becd', hid, w_down) * g[..., None].astype(h.dtype)
    return jax.vmap(lambda yb, ib: jnp.zeros((t, d), yb.dtype).at[ib.reshape(-1)].add(yb.reshape(-1, d)))(ye, idx)


def layer(xl, xc, c, c_ctx, lp, rope, last):
    mod_l = (jax.nn.silu(c) @ lp['w_mod'] + lp['b_mod'])[:, None, :]
    mod_c = (jax.nn.silu(c_ctx) @ lp['w_mod'] + lp['b_mod'])[None, None, :]
    sh1_l, sc1_l, g1_l, sh2_l, sc2_l, g2_l = jnp.split(mod_l, 6, axis=-1)
    sh1_c, sc1_c, g1_c, sh2_c, sc2_c, g2_c = jnp.split(mod_c, 6, axis=-1)
    hl = modulate(rmsnorm(xl, lp['norm1_g']), sh1_l, sc1_l)
    hc = modulate(rmsnorm(xc, lp['norm1_g']), sh1_c, sc1_c)
    za_l, xb_l, gb_l, cq_l, ckv_l, kr_l = jnp.split(hl @ lp['w_in'], IN_SPLITS, axis=-1)
    za_c, xb_c, gb_c, cq_c, ckv_c, kr_c = jnp.split(hc @ lp['w_in'], IN_SPLITS, axis=-1)
    kc, vc = mla_kv(ckv_c, kr_c, lp['kv_norm_g'], lp['w_ukv'], None)
    kl, vl = mla_kv(ckv_l, kr_l, lp['kv_norm_g'], lp['w_ukv'], rope)
    ql = mla_q(cq_l, lp['q_norm_g'], lp['w_uq'], rope)
    yc_l = latent_attention(ql, jnp.concatenate([kc, kl], axis=1), jnp.concatenate([vc, vl], axis=1))
    yb_l, yb_c = rglru_mixer(xb_l, gb_l, xb_c, gb_c, lp, not last)
    ya_l = chunk_sgu(jax.nn.gelu(za_l), lp['sgu_w'], lp['sgu_b'])
    xl = xl + g1_l * (jnp.concatenate([ya_l, yb_l, yc_l], axis=-1) @ lp['w_out'])
    h2l = modulate(rmsnorm(xl, lp['norm2_g']), sh2_l, sc2_l)
    xl = xl + g2_l * ec_moe(h2l, lp['w_router'], lp['w_gate'], lp['w_up'], lp['w_down'])
    if not last:
        b = xc.shape[0]
        qc = mla_q(cq_c, lp['q_norm_g'], lp['w_uq'], None)
        yc_c = attend(qc, kc, vc).reshape(b, xc.shape[1], C_WIDTH)
        ya_c = chunk_sgu(jax.nn.gelu(za_c), lp['sgu_w'], lp['sgu_b'])
        xc = xc + g1_c * (jnp.concatenate([ya_c, yb_c, yc_c], axis=-1) @ lp['w_out'])
        h2c = modulate(rmsnorm(xc, lp['norm2_g']), sh2_c, sc2_c)
        xc = xc + g2_c * ec_moe(h2c, lp['w_router'], lp['w_gate'], lp['w_up'], lp['w_down'])
    return xl, xc


def setup_inputs(seed: int = 0) -> dict:
    key = jax.random.key(seed)
    ks = jax.random.split(key, 32)
    L, D = DEPTH, D_MODEL
    f32 = jnp.float32

    def nrm(k, shape, s):
        return jax.random.normal(k, shape, f32) * s

    def gain(k, shape):
        return 1.0 + nrm(k, shape, 0.01)

    u = jax.random.uniform(ks[16], (L, 2, B_WIDTH), f32, minval=0.9, maxval=0.999)
    base = u ** (1.0 / LRU_C)
    lru_lambda = jnp.log(base) - jnp.log1p(-base)
    bw = B_WIDTH // B_BLOCKS
    return {
        'x': nrm(ks[0], (BATCH, SEQ, D), 1.0),
        'c': nrm(ks[1], (BATCH, D), 1.0),
        'ctx': nrm(ks[2], (BATCH, CTX_LEN, D), 1.0),
        'c_ctx': nrm(ks[3], (D,), 1.0),
        'norm1_g': gain(ks[4], (L, D)),
        'w_mod': nrm(ks[5], (L, D, 6 * D), 0.5 * D ** -0.5),
        'b_mod': nrm(ks[6], (L, 6 * D), 0.01),
        'w_in': nrm(ks[7], (L, D, IN_COLS), D ** -0.5),
        'sgu_w': nrm(ks[8], (L, A_GROUPS, CHUNK, CHUNK), 0.5 * CHUNK ** -0.5),
        'sgu_b': 1.0 + nrm(ks[9], (L, A_GROUPS, CHUNK), 0.01),
        'conv_w': nrm(ks[10], (L, CONV_W, B_WIDTH), CONV_W ** -0.5),
        'conv_b': nrm(ks[11], (L, B_WIDTH), 0.01),
        'lru_wa': nrm(ks[12], (L, 2, B_BLOCKS, bw, bw), bw ** -0.5),
        'lru_ba': nrm(ks[13], (L, 2, B_WIDTH), 0.01),
        'lru_wx': nrm(ks[14], (L, 2, B_BLOCKS, bw, bw), bw ** -0.5),
        'lru_bx': nrm(ks[15], (L, 2, B_WIDTH), 0.01),
        'lru_lambda': lru_lambda,
        'q_norm_g': gain(ks[17], (L, Q_LORA)),
        'w_uq': nrm(ks[18], (L, Q_LORA, C_HEADS * (C_NOPE + C_ROPE)), Q_LORA ** -0.5),
        'kv_norm_g': gain(ks[19], (L, KV_LORA)),
        'w_ukv': nrm(ks[20], (L, KV_LORA, C_HEADS * (C_NOPE + C_V)), KV_LORA ** -0.5),
        'w_out': nrm(ks[21], (L, MIX_WIDTH, D), MIX_WIDTH ** -0.5),
        'norm2_g': gain(ks[22], (L, D)),
        'w_router': nrm(ks[23], (L, D, N_EXPERTS), D ** -0.5),
        'w_gate': nrm(ks[24], (L, N_EXPERTS, D, EXPERT_FF), D ** -0.5),
        'w_up': nrm(ks[25], (L, N_EXPERTS, D, EXPERT_FF), D ** -0.5),
        'w_down': nrm(ks[26], (L, N_EXPERTS, EXPERT_FF, D), EXPERT_FF ** -0.5),
        'final_norm_g': gain(ks[27], (D,)),
    }


def reference(x, c, ctx, c_ctx, norm1_g, w_mod, b_mod, w_in, sgu_w, sgu_b, conv_w, conv_b, lru_wa, lru_ba, lru_wx, lru_bx, lru_lambda, q_norm_g, w_uq, kv_norm_g, w_ukv, w_out, norm2_g, w_router, w_gate, w_up, w_down, final_norm_g):
    rope = rope_tables(x.shape[1])
    xl, xc = x, ctx
    for i in range(DEPTH):
        lp = {
            'norm1_g': norm1_g[i], 'w_mod': w_mod[i], 'b_mod': b_mod[i], 'w_in': w_in[i],
            'sgu_w': sgu_w[i], 'sgu_b': sgu_b[i], 'conv_w': conv_w[i], 'conv_b': conv_b[i],
            'lru_wa': lru_wa[i], 'lru_ba': lru_ba[i], 'lru_wx': lru_wx[i], 'lru_bx': lru_bx[i],
            'lru_lambda': lru_lambda[i], 'q_norm_g': q_norm_g[i], 'w_uq': w_uq[i],
            'kv_norm_g': kv_norm_g[i], 'w_ukv': w_ukv[i], 'w_out': w_out[i], 'norm2_g': norm2_g[i],
            'w_router': w_router[i], 'w_gate': w_gate[i], 'w_up': w_up[i], 'w_down': w_down[i],
        }
        xl, xc = layer(xl, xc, c, c_ctx, lp, rope, i == DEPTH - 1)
    return rmsnorm(xl, final_norm_g)
```

```python
import functools
import math

import jax
import jax.numpy as jnp
import jax.scipy.linalg
from jax import lax
from jax.experimental import pallas as pl
from jax.experimental.pallas import tpu as pltpu

F32 = jnp.float32
BF16 = jnp.bfloat16
HIGHEST = lax.Precision.HIGHEST

GRID_W = 64
CHUNK = 128
EPS = 1e-6
A_GROUPS = 4
B_BLOCKS = 4
CONV_W = 4
LRU_C = 8.0
C_NOPE = 128
C_ROPE = 64
C_V = 128
ROPE_BASE = 10000.0
EC_CAPACITY = 2
LOG2E = 1.4426950408889634
SMALLEST_NORMAL = 1.1754943508222875e-38
BISECT_STEPS = 48

TOKEN_TILE = 256
LANES = 128
Q_TILE = 1024
Q_SUBTILE = 256
KV_CHUNK = 1280
SLOT_BLOCK = 256
FFN_ROWS = 512
VMEM_LIMIT = 60 * 1024 * 1024


def _cparams(sem, vmem=None):
    return pltpu.CompilerParams(dimension_semantics=sem, vmem_limit_bytes=vmem)


def _rms(x):
    return x * lax.rsqrt(jnp.mean(x * x, axis=-1, keepdims=True) + EPS)


def _mod_kernel(cv_ref, w_ref, b_ref, o_ref):
    cv = cv_ref[...]
    s = cv * jax.nn.sigmoid(cv)
    o_ref[...] = jnp.dot(s, w_ref[...], precision=HIGHEST, preferred_element_type=F32) + b_ref[...]


def _mod_call(cv, w_mod, b_mod):
    depth, d, d6 = w_mod.shape
    tn = 1536
    return pl.pallas_call(
        _mod_kernel,
        grid=(depth, d6 // tn),
        in_specs=[
            pl.BlockSpec((8, d), lambda l, n: (0, 0)),
            pl.BlockSpec((None, d, tn), lambda l, n: (l, 0, n)),
            pl.BlockSpec((None, 1, tn), lambda l, n: (l, 0, n)),
        ],
        out_specs=pl.BlockSpec((None, 8, tn), lambda l, n: (l, 0, n)),
        out_shape=jax.ShapeDtypeStruct((depth, 8, d6), F32),
        compiler_params=_cparams(("arbitrary", "arbitrary")),
        name="modulation",
    )(cv, w_mod, b_mod.reshape(depth, 1, d6))


def _in_kernel(x_ref, mod_ref, g1_ref, win_ref, sguw_ref, sgub_ref, qg_ref, wuq_ref, kvg_ref, wukv_ref, cos_ref, sin_ref,
               ya_ref, xb_ref, gb_ref, q_ref, k_ref, v_ref, *, n_lat_tiles, qscale):
    tm, d = x_ref.shape
    is_ctx = pl.program_id(0) >= n_lat_tiles
    mod = mod_ref[...]
    m = jnp.where(is_ctx, mod[1:2], mod[0:1])
    shift, scale = m[:, 0:d], m[:, d:2 * d]
    h = _rms(x_ref[...]) * g1_ref[...]
    h = h * (1.0 + scale) + shift
    z = jnp.dot(h.astype(BF16), win_ref[...], preferred_element_type=F32)

    aw = sgub_ref.shape[1]
    za = jax.nn.gelu(z[:, 0:2 * aw])
    u, val = za[:, :aw], za[:, aw:]
    vb = _rms(val).astype(BF16)
    gw = aw // A_GROUPS
    lane = lax.broadcasted_iota(jnp.int32, (CHUNK, aw), 1)
    for ci in range(tm // CHUNK):
        vc = vb[ci * CHUNK:(ci + 1) * CHUNK]
        vexp = jnp.concatenate(
            [jnp.where((lane >= g * gw) & (lane < (g + 1) * gw), vc, jnp.zeros_like(vc)) for g in range(A_GROUPS)], axis=0)
        mixed = jnp.dot(sguw_ref[...], vexp, preferred_element_type=F32) + sgub_ref[...]
        ya_ref[ci * CHUNK:(ci + 1) * CHUNK, :] = (u[ci * CHUNK:(ci + 1) * CHUNK] * mixed).astype(ya_ref.dtype)

    o = 2 * aw
    bw = xb_ref.shape[1]
    xb_ref[...] = z[:, o:o + bw]
    gb_ref[...] = jax.nn.gelu(z[:, o + bw:o + 2 * bw])
    o += 2 * bw

    ql = qg_ref.shape[1]
    kvl = kvg_ref.shape[1]
    nh = q_ref.shape[0]
    cos_t, sin_t = cos_ref[...], sin_ref[...]
    cqn = _rms(z[:, o:o + ql]) * qg_ref[...]
    qf = jnp.dot(cqn.astype(BF16), wuq_ref[...], preferred_element_type=F32)
    nn, nr = nh * C_NOPE, nh * C_ROPE
    qr = qf[:, nn:nn + nr] * cos_t + qf[:, nn + nr:nn + 2 * nr] * sin_t
    o += ql
    ckvn = _rms(z[:, o:o + kvl]) * kvg_ref[...]
    kvf = jnp.dot(ckvn.astype(BF16), wukv_ref[...], preferred_element_type=F32)
    o += kvl
    kr = z[:, o:o + C_ROPE] * cos_t[:, :C_ROPE] + z[:, o + C_ROPE:o + 2 * C_ROPE] * sin_t[:, :C_ROPE]
    krb = kr.astype(k_ref.dtype)
    for hh in range(nh):
        q_ref[hh, :, 0:C_NOPE] = (qf[:, hh * C_NOPE:(hh + 1) * C_NOPE] * qscale).astype(q_ref.dtype)
        q_ref[hh, :, C_NOPE:C_NOPE + C_ROPE] = (qr[:, hh * C_ROPE:(hh + 1) * C_ROPE] * qscale).astype(q_ref.dtype)
        k_ref[hh, :, 0:C_NOPE] = kvf[:, hh * C_NOPE:(hh + 1) * C_NOPE].astype(k_ref.dtype)
        k_ref[hh, :, C_NOPE:C_NOPE + C_ROPE] = krb
        v_ref[hh] = kvf[:, nn + hh * C_V:nn + (hh + 1) * C_V].astype(v_ref.dtype)


def _in_call(xall, mod, g1, win, sguw, sgub, qg, wuq, kvg, wukv, cos_t, sin_t, *, n_lat_tiles, nh):
    n, d = xall.shape
    tm = TOKEN_TILE
    aw = sgub.shape[1]
    bw = aw
    dk = C_NOPE + C_ROPE
    full = lambda a: pl.BlockSpec(a.shape, lambda i: (0,) * a.ndim)
    row = lambda w: pl.BlockSpec((tm, w), lambda i: (i, 0))
    kern = functools.partial(_in_kernel, n_lat_tiles=n_lat_tiles, qscale=float(dk ** -0.5 * LOG2E))
    return pl.pallas_call(
        kern,
        grid=(n // tm,),
        in_specs=[row(d), full(mod), full(g1), full(win), full(sguw), full(sgub), full(qg), full(wuq), full(kvg), full(wukv),
                  row(cos_t.shape[1]), row(sin_t.shape[1])],
        out_specs=[row(aw), row(bw), row(bw),
                   pl.BlockSpec((nh, tm, dk), lambda i: (0, i, 0)),
                   pl.BlockSpec((nh, tm, dk), lambda i: (0, i, 0)),
                   pl.BlockSpec((nh, tm, C_V), lambda i: (0, i, 0))],
        out_shape=[jax.ShapeDtypeStruct((n, aw), BF16), jax.ShapeDtypeStruct((n, bw), F32), jax.ShapeDtypeStruct((n, bw), F32),
                   jax.ShapeDtypeStruct((nh, n, dk), BF16), jax.ShapeDtypeStruct((nh, n, dk), BF16),
                   jax.ShapeDtypeStruct((nh, n, C_V), BF16)],
        compiler_params=_cparams(("arbitrary",), VMEM_LIMIT),
        name="input_proj",
    )(xall, mod, g1, win, sguw, sgub, qg, wuq, kvg, wukv, cos_t, sin_t)


def _tile_scan(a, b, reverse):
    n = a.shape[0]
    rows = lax.broadcasted_iota(jnp.int32, a.shape, 0)
    k = 1
    while k < n:
        sh = n - k if reverse else k
        valid = (rows < n - k) if reverse else (rows >= k)
        a_s = pltpu.roll(a, sh, 0)
        b_s = pltpu.roll(b, sh, 0)
        b = jnp.where(valid, a * b_s, 0.0) + b
        a = jnp.where(valid, a * a_s, a)
        k *= 2
    return a, b


def _softplus(x):
    return jnp.maximum(x, 0.0) + jnp.log(1.0 + jnp.exp(-jnp.abs(x)))


def _lru_kernel(*refs, reverse, nt):
    if reverse:
        xp_ref, x_ref, xn_ref, hf_ref, gb_ref, cw_ref, cb_ref, wa_ref, ba_ref, wx_ref, bx_ref, lam_ref, o_ref, carry_ref = refs
    else:
        xp_ref, x_ref, xn_ref, cw_ref, cb_ref, wa_ref, ba_ref, wx_ref, bx_ref, lam_ref, o_ref, carry_ref = refs
    tm = x_ref.shape[0]
    s = pl.program_id(0)
    tile = _lru_tile(s, nt, reverse)

    @pl.when(s == 0)
    def _():
        carry_ref[...] = jnp.zeros_like(carry_ref)

    prev_ok = jnp.logical_and(tile != 0, tile != nt - 1)
    next_ok = tile < nt - 2
    xp = jnp.where(prev_ok, xp_ref[...], 0.0)
    xn = jnp.where(next_ok, xn_ref[...], 0.0)
    ext = jnp.concatenate([xp, x_ref[...], xn], axis=0)
    cw = cw_ref[...]
    conv = cb_ref[...]
    for kk in range(CONV_W):
        st = 8 + kk - CONV_W // 2
        conv = conv + ext[st:st + tm] * cw[kk:kk + 1]

    cb16 = conv.astype(BF16)
    r = jax.nn.sigmoid(jnp.dot(cb16, wa_ref[...], preferred_element_type=F32) + ba_ref[...])
    gate = jax.nn.sigmoid(jnp.dot(cb16, wx_ref[...], preferred_element_type=F32) + bx_ref[...])
    log_a = (-LRU_C) * r * _softplus(-lam_ref[...])
    a = jnp.exp(log_a)
    mult = jnp.sqrt(jnp.maximum(-jnp.tanh(log_a) * (a * a + 1.0), 0.0))
    b = mult * (gate * conv)
    acum, bcum = _tile_scan(a, b, reverse)
    h = acum * carry_ref[...] + bcum
    carry_ref[...] = h[0:1] if reverse else h[tm - 1:tm]
    if reverse:
        o_ref[...] = ((hf_ref[...] + h) * gb_ref[...]).astype(o_ref.dtype)
    else:
        o_ref[...] = h


def _lru_tile(s, nt, reverse):
    if reverse:
        return jnp.where(s == 0, nt - 1, nt - 1 - s)
    return jnp.where(s == 0, nt - 1, s - 1)


def _lru_call(xb, hf, gb, cw, cb, wa, ba, wx, bx, lam, *, reverse):
    n, w = xb.shape
    tm = TOKEN_TILE
    nt = n // tm
    r8 = tm // 8
    tile = lambda s: _lru_tile(s, nt, reverse)
    row = pl.BlockSpec((tm, w), lambda s: (tile(s), 0))
    prev = pl.BlockSpec((8, w), lambda s: (jnp.maximum(tile(s) * r8 - 1, 0), 0))
    nxt = pl.BlockSpec((8, w), lambda s: (jnp.minimum((tile(s) + 1) * r8, n // 8 - 1), 0))
    full = lambda a: pl.BlockSpec(a.shape, lambda s: (0,) * a.ndim)
    params = [cw, cb, wa, ba, wx, bx, lam]
    ins = [xb, xb, xb] + ([hf, gb] if reverse else []) + params
    specs = [prev, row, nxt] + ([row, row] if reverse else []) + [full(p) for p in params]
    return pl.pallas_call(
        functools.partial(_lru_kernel, reverse=reverse, nt=nt),
        grid=(nt,),
        in_specs=specs,
        out_specs=row,
        out_shape=jax.ShapeDtypeStruct((n, w), BF16 if reverse else F32),
        scratch_shapes=[pltpu.VMEM((1, w), F32)],
        compiler_params=_cparams(("arbitrary",)),
        name="lru_bwd" if reverse else "lru_fwd",
    )(*ins)


def _attn_kernel(*refs, tk, nk, ts, aliased):
    if aliased:
        refs = refs[1:]
    q_ref, k_ref, v_ref, o_ref = refs[:4]
    scratch = refs[4:]
    nsub = q_ref.shape[0] // ts
    m_refs, l_refs, acc_refs = scratch[:nsub], scratch[nsub:2 * nsub], scratch[2 * nsub:]
    lanes = m_refs[0].shape[1]
    nblk = tk // lanes
    for u in range(nsub):
        m_refs[u][...] = jnp.full(m_refs[u].shape, -jnp.inf, F32)
        l_refs[u][...] = jnp.zeros(l_refs[u].shape, F32)
        acc_refs[u][...] = jnp.zeros(acc_refs[u].shape, F32)

    def body(c, carry):
        off = pl.multiple_of(c * tk, tk)
        k = k_ref[pl.ds(off, tk), :]
        v = v_ref[pl.ds(off, tk), :]
        for u in range(nsub):
            q = q_ref[u * ts:(u + 1) * ts, :]
            s = lax.dot_general(q, k, (((1,), (1,)), ((), ())), preferred_element_type=F32)
            m_prev = m_refs[u][...]
            m_new = jnp.maximum(m_prev, jnp.max(s, axis=-1, keepdims=True))
            alpha = jnp.exp2(m_prev - m_new)
            pb = [jnp.exp2(s[:, b * lanes:(b + 1) * lanes] - m_new) for b in range(nblk)]
            psum = pb[0]
            for b in range(1, nblk):
                psum = psum + pb[b]
            l_refs[u][...] = alpha * l_refs[u][...] + psum
            p = jnp.concatenate(pb, axis=1).astype(v.dtype)
            acc_refs[u][...] = alpha * acc_refs[u][...] + jnp.dot(p, v, preferred_element_type=F32)
            m_refs[u][...] = m_new
        return carry

    lax.fori_loop(0, nk, body, 0)
    for u in range(nsub):
        l = jnp.sum(l_refs[u][...], axis=-1, keepdims=True)
        o_ref[u * ts:(u + 1) * ts, :] = (acc_refs[u][...] / l).astype(o_ref.dtype)


def _attn_call(q, k, v, yc_prev, *, q_row0, nq_rows, kv_row0, kv_len, tq, tk, n_out):
    nh, _, dk = q.shape
    dv = v.shape[2]
    ts = min(tq, Q_SUBTILE)
    assert q_row0 % tq == 0 and nq_rows % tq == 0 and kv_row0 % kv_len == 0 and kv_len % tk == 0 and tq % ts == 0
    assert dv == LANES and tk % LANES == 0
    nsub = tq // ts
    qb0, kvb = q_row0 // tq, kv_row0 // kv_len
    aliased = yc_prev is not None
    in_specs = [pl.BlockSpec((None, tq, dk), lambda h, i: (h, qb0 + i, 0)),
                pl.BlockSpec((None, kv_len, dk), lambda h, i: (h, kvb, 0)),
                pl.BlockSpec((None, kv_len, dv), lambda h, i: (h, kvb, 0))]
    ins = [q, k, v]
    if aliased:
        in_specs = [pl.BlockSpec(memory_space=pl.ANY)] + in_specs
        ins = [yc_prev] + ins
    return pl.pallas_call(
        functools.partial(_attn_kernel, tk=tk, nk=kv_len // tk, ts=ts, aliased=aliased),
        grid=(nh, nq_rows // tq),
        in_specs=in_specs,
        out_specs=pl.BlockSpec((tq, dv), lambda h, i: (qb0 + i, h)),
        out_shape=jax.ShapeDtypeStruct((n_out, nh * dv), BF16),
        scratch_shapes=[pltpu.VMEM((ts, LANES), F32)] * (3 * nsub),
        input_output_aliases={0: 0} if aliased else {},
        compiler_params=_cparams(("arbitrary", "arbitrary"), VMEM_LIMIT),
        name="attention",
    )(*ins)


def _out_kernel(x_ref, ya_ref, yb_ref, yc_ref, wout_ref, mod_ref, g2_ref, wr_ref, x1_ref, h2_ref, p_ref, *, n_lat_tiles):
    d = x_ref.shape[1]
    is_ctx = pl.program_id(0) >= n_lat_tiles
    mod = mod_ref[...]
    m = jnp.where(is_ctx, mod[1:2], mod[0:1])
    gate1, shift2, scale2 = m[:, 2 * d:3 * d], m[:, 3 * d:4 * d], m[:, 4 * d:5 * d]
    mix = jnp.concatenate([ya_ref[...], yb_ref[...], yc_ref[...]], axis=-1)
    x1 = x_ref[...] + gate1 * jnp.dot(mix, wout_ref[...], preferred_element_type=F32)
    x1_ref[...] = x1
    h2 = _rms(x1) * g2_ref[...]
    h2 = h2 * (1.0 + scale2) + shift2
    h2_ref[...] = h2.astype(h2_ref.dtype)
    logits = jnp.dot(h2, wr_ref[...], precision=HIGHEST, preferred_element_type=F32)
    ex = jnp.exp(logits - jnp.max(logits, axis=-1, keepdims=True))
    p_ref[...] = ex / jnp.sum(ex, axis=-1, keepdims=True)


def _out_call(xall, ya, yb, yc, wout, mod, g2, wr, *, n_tiles, n_lat_tiles):
    d = xall.shape[1]
    tm = TOKEN_TILE
    n = n_tiles * tm
    ne = wr.shape[1]
    full = lambda a: pl.BlockSpec(a.shape, lambda i: (0,) * a.ndim)
    row = lambda w: pl.BlockSpec((tm, w), lambda i: (i, 0))
    return pl.pallas_call(
        functools.partial(_out_kernel, n_lat_tiles=n_lat_tiles),
        grid=(n_tiles,),
        in_specs=[row(d), row(ya.shape[1]), row(yb.shape[1]), row(yc.shape[1]), full(wout), full(mod), full(g2), full(wr)],
        out_specs=[row(d), row(d), row(ne)],
        out_shape=[jax.ShapeDtypeStruct((n, d), F32), jax.ShapeDtypeStruct((n, d), BF16), jax.ShapeDtypeStruct((n, ne), F32)],
        compiler_params=_cparams(("arbitrary",), VMEM_LIMIT),
        name="output_proj",
    )(xall, ya, yb, yc, wout, mod, g2, wr)


def _cumsum_lanes(x):
    n = x.shape[1]
    lane = lax.broadcasted_iota(jnp.int32, x.shape, 1)
    k = 1
    while k < n:
        x = x + jnp.where(lane >= k, pltpu.roll(x, k, 1), 0)
        k *= 2
    return x


def _count(mask):
    return jnp.sum(jnp.where(mask, 1.0, 0.0), axis=1, keepdims=True).astype(jnp.int32)


def _select_kernel(p_ref, pos_ref, excl_ref, *, cap):
    p = p_ref[...]

    def step(i, bounds):
        lo, hi = bounds
        lo_pos = jnp.maximum(lo, SMALLEST_NORMAL)
        mid = jnp.where(hi > 2.0 * lo_pos, jnp.sqrt(lo_pos) * jnp.sqrt(hi), lo + 0.5 * (hi - lo))
        mid = jnp.clip(mid, lo, hi)
        ok = _count(p >= mid) >= cap
        return jnp.where(ok, mid, lo), jnp.where(ok, hi, mid)

    rows = (p.shape[0], 1)
    lo, hi = lax.fori_loop(0, BISECT_STEPS, step, (jnp.zeros(rows, F32), jnp.full(rows, 2.0, F32)))
    gt = p >= hi
    eq = jnp.logical_and(p >= lo, p < hi)
    need = cap - _count(gt)
    eqi = jnp.where(eq, 1, 0)
    eq_rank = _cumsum_lanes(eqi) - eqi
    sel = jnp.logical_or(gt, jnp.logical_and(eq, eq_rank < need))
    seli = jnp.where(sel, 1, 0)
    excl = _cumsum_lanes(seli) - seli
    excl_ref[...] = excl
    pos_ref[...] = jnp.where(sel, excl, -1)


def _select_call(p_t, cap):
    ne, t = p_t.shape
    return pl.pallas_call(
        functools.partial(_select_kernel, cap=cap),
        out_shape=[jax.ShapeDtypeStruct((ne, t), jnp.int32), jax.ShapeDtypeStruct((ne, t), jnp.int32)],
        compiler_params=pltpu.CompilerParams(vmem_limit_bytes=VMEM_LIMIT),
        name="expert_select",
    )(p_t)


def _moe_ffn_kernel(c0_ref, pos_ref, h_ref, wg_ref, wu_ref, wd_ref, y_ref, xs_ref, *, nb, win, cap, rows):
    e = pl.program_id(0)
    j = pl.program_id(1)
    tm = h_ref.shape[0]

    @pl.when(j == 0)
    def _():
        xs_ref[...] = jnp.zeros_like(xs_ref)

    c0 = c0_ref[e * nb + j]
    w0 = pl.multiple_of((c0 // 8) * 8, 8)
    rel = pos_ref[...] - w0
    slot = lax.broadcasted_iota(jnp.int32, (win, tm), 0)
    onehot = jnp.where(slot == rel, 1.0, 0.0).astype(BF16)
    xs_ref[pl.ds(w0, win), :] += jnp.dot(onehot, h_ref[...], preferred_element_type=F32)

    @pl.when(j == nb - 1)
    def _():
        wg = wg_ref[...].astype(BF16)
        wu = wu_ref[...].astype(BF16)
        wd = wd_ref[...].astype(BF16)
        for ci in range(cap // rows):
            xc = xs_ref[ci * rows:(ci + 1) * rows, :].astype(BF16)
            hg = jnp.dot(xc, wg, preferred_element_type=F32)
            hu = jnp.dot(xc, wu, preferred_element_type=F32)
            hid = (hg * jax.nn.sigmoid(hg) * hu).astype(BF16)
            y_ref[ci * rows:(ci + 1) * rows, :] = jnp.dot(hid, wd, preferred_element_type=F32).astype(y_ref.dtype)


def _moe_ffn_call(c0, pos_t, h2, wg, wu, wd, *, cap, tile0):
    ne, t = pos_t.shape
    tm = TOKEN_TILE
    nb = t // tm
    d = h2.shape[1]
    ff = wg.shape[2]
    win = min(tm, cap) + 16
    rows = min(cap, FFN_ROWS)
    return pl.pallas_call(
        functools.partial(_moe_ffn_kernel, nb=nb, win=win, cap=cap, rows=rows),
        grid_spec=pltpu.PrefetchScalarGridSpec(
            num_scalar_prefetch=1,
            grid=(ne, nb),
            in_specs=[pl.BlockSpec((None, 1, tm), lambda e, j, c: (e, 0, j)),
                      pl.BlockSpec((tm, d), lambda e, j, c: (tile0 + j, 0)),
                      pl.BlockSpec((None, d, ff), lambda e, j, c: (e, 0, 0)),
                      pl.BlockSpec((None, d, ff), lambda e, j, c: (e, 0, 0)),
                      pl.BlockSpec((None, ff, d), lambda e, j, c: (e, 0, 0))],
            out_specs=pl.BlockSpec((None, cap, d), lambda e, j, c: (e, 0, 0)),
            scratch_shapes=[pltpu.VMEM((cap + win, d), F32)]),
        out_shape=jax.ShapeDtypeStruct((ne, cap, d), BF16),
        compiler_params=_cparams(("arbitrary", "arbitrary"), VMEM_LIMIT),
        name="moe_ffn",
    )(c0, pos_t.reshape(ne, 1, t), h2, wg, wu, wd)


def _combine_kernel(*refs, ne, sb, nsb, n_lat_tiles, tile0, final, aliased):
    refs = list(refs)
    b0_ref = refs.pop(0)
    if aliased:
        refs.pop(0)
    y0_ref, y1_ref, pos_ref, p_ref, x1_ref, mod_ref = refs[:6]
    refs = refs[6:]
    fg_ref = refs.pop(0) if final else None
    o_ref, acc_ref = refs
    j = pl.program_id(0)
    e = pl.program_id(1)
    tm, d = x1_ref.shape

    @pl.when(e == 0)
    def _():
        acc_ref[...] = jnp.zeros_like(acc_ref)

    col = lax.broadcasted_iota(jnp.int32, pos_ref.shape, 1) == e
    slot_of = jnp.sum(jnp.where(col, pos_ref[...].astype(F32), 0.0), axis=1, keepdims=True).astype(jnp.int32)
    gate = jnp.sum(jnp.where(col, p_ref[...], 0.0), axis=1, keepdims=True)
    b0 = b0_ref[j * ne + e]
    b1 = jnp.minimum(b0 + 1, nsb - 1)
    slot = lax.broadcasted_iota(jnp.int32, (tm, sb), 1)
    hit0 = slot == slot_of - b0 * sb
    hit1 = jnp.logical_and(slot == slot_of - b1 * sb, b1 != b0)
    r = jnp.dot(jnp.where(hit0, 1.0, 0.0).astype(BF16), y0_ref[...], preferred_element_type=F32)
    r = r + jnp.dot(jnp.where(hit1, 1.0, 0.0).astype(BF16), y1_ref[...], preferred_element_type=F32)
    acc_ref[...] += gate * r

    @pl.when(e == ne - 1)
    def _():
        mod = mod_ref[...]
        m = jnp.where(tile0 + j >= n_lat_tiles, mod[1:2], mod[0:1])
        out = x1_ref[...] + m[:, 5 * d:6 * d] * acc_ref[...]
        if final:
            out = _rms(out) * fg_ref[...]
        o_ref[...] = out


def _combine_call(b0, y, pos, probs, x1, mod, fg, prev, *, cap, tile0, n_lat_tiles, n_out):
    ne = y.shape[0]
    t = pos.shape[0]
    tm = TOKEN_TILE
    nb = t // tm
    d = x1.shape[1]
    sb = min(SLOT_BLOCK, cap)
    nsb = cap // sb
    final = fg is not None
    aliased = prev is not None
    full = lambda a: pl.BlockSpec(a.shape, lambda j, e, b: (0,) * a.ndim)
    in_specs = [pl.BlockSpec((None, sb, d), lambda j, e, b: (e, b[j * ne + e], 0)),
                pl.BlockSpec((None, sb, d), lambda j, e, b: (e, jnp.minimum(b[j * ne + e] + 1, nsb - 1), 0)),
                pl.BlockSpec((tm, ne), lambda j, e, b: (j, 0)),
                pl.BlockSpec((tm, ne), lambda j, e, b: (j, 0)),
                pl.BlockSpec((tm, d), lambda j, e, b: (tile0 + j, 0)),
                full(mod)]
    ins = [y, y, pos, probs, x1, mod]
    if final:
        in_specs.append(full(fg))
        ins.append(fg)
    if aliased:
        in_specs = [pl.BlockSpec(memory_space=pl.ANY)] + in_specs
        ins = [prev] + ins
    out_tile0 = 0 if final else tile0
    return pl.pallas_call(
        functools.partial(_combine_kernel, ne=ne, sb=sb, nsb=nsb, n_lat_tiles=n_lat_tiles, tile0=tile0, final=final,
                          aliased=aliased),
        grid_spec=pltpu.PrefetchScalarGridSpec(
            num_scalar_prefetch=1,
            grid=(nb, ne),
            in_specs=in_specs,
            out_specs=pl.BlockSpec((tm, d), lambda j, e, b: (out_tile0 + j, 0)),
            scratch_shapes=[pltpu.VMEM((tm, d), F32)]),
        out_shape=jax.ShapeDtypeStruct((n_out, d), F32),
        input_output_aliases={1: 0} if aliased else {},
        compiler_params=_cparams(("arbitrary", "arbitrary"), VMEM_LIMIT),
        name="moe_combine",
    )(b0, *ins)


def _rope_tables(t, t_ctx, nh):
    pos = jnp.arange(t)
    r = (pos // GRID_W).astype(F32)
    col = (pos % GRID_W).astype(F32)
    nf = C_ROPE // 4
    inv = ROPE_BASE ** (-jnp.arange(nf, dtype=F32) / nf)
    ar, ac = r[:, None] * inv, col[:, None] * inv
    cos_t = jnp.concatenate([jnp.cos(ar), jnp.cos(ar), jnp.cos(ac), jnp.cos(ac)], axis=-1)
    sin_t = jnp.concatenate([-jnp.sin(ar), jnp.sin(ar), -jnp.sin(ac), jnp.sin(ac)], axis=-1)
    cos_t = jnp.concatenate([cos_t, jnp.ones((t_ctx, C_ROPE), F32)], axis=0)
    sin_t = jnp.concatenate([sin_t, jnp.zeros((t_ctx, C_ROPE), F32)], axis=0)
    return jnp.tile(cos_t, (1, nh)), jnp.tile(sin_t, (1, nh))


def _swap_perm():
    nf = C_ROPE // 4
    idx = jnp.arange(C_ROPE).reshape(2, 2, nf)
    return idx[:, ::-1, :].reshape(-1)


def _moe(h2, probs_t, probs, x1, mod, fg, prev, lp, *, t, tile0, n_lat_tiles, n_out):
    ne = probs_t.shape[0]
    tm = TOKEN_TILE
    cap = EC_CAPACITY * t // ne
    pos_t, excl = _select_call(probs_t, cap)
    c0 = excl[:, ::tm]
    y = _moe_ffn_call(c0.reshape(-1), pos_t, h2, lp['w_gate'], lp['w_up'], lp['w_down'], cap=cap, tile0=tile0)
    sb = min(SLOT_BLOCK, cap)
    b0 = jnp.minimum(c0 // sb, cap // sb - 1).T.reshape(-1)
    return _combine_call(b0, y, pos_t.T, probs, x1, mod, fg, prev, cap=cap, tile0=tile0, n_lat_tiles=n_lat_tiles, n_out=n_out)


def kernel(x, c, ctx, c_ctx, norm1_g, w_mod, b_mod, w_in, sgu_w, sgu_b, conv_w, conv_b, lru_wa, lru_ba, lru_wx, lru_bx, lru_lambda, q_norm_g, w_uq, kv_norm_g, w_ukv, w_out, norm2_g, w_router, w_gate, w_up, w_down, final_norm_g):
    bsz, t, d = x.shape
    t_ctx = ctx.shape[1]
    depth = w_mod.shape[0]
    tm = TOKEN_TILE
    assert bsz == 1 and t_ctx == tm and t % Q_TILE == 0 and t % GRID_W == 0
    n = t + t_ctx
    nlt = t // tm
    nt = n // tm
    ne = w_router.shape[2]
    ql, kvl = q_norm_g.shape[1], kv_norm_g.shape[1]
    nh = w_uq.shape[2] // (C_NOPE + C_ROPE)
    aw = bw = (w_in.shape[2] - ql - kvl - C_ROPE) // 4
    assert (n % KV_CHUNK) == 0

    xall = jnp.concatenate([x[0], ctx[0]], axis=0)
    cv = jnp.zeros((8, d), F32).at[0].set(c[0]).at[1].set(c_ctx)
    mods = _mod_call(cv, w_mod, b_mod)[:, :2]
    cos_t, sin_t = _rope_tables(t, t_ctx, nh)
    perm = _swap_perm()
    row2 = lambda a: a.reshape(1, -1)

    out = None
    for li in range(depth):
        last = li == depth - 1
        mod = mods[li]
        wi = w_in[li]
        o_kr = 2 * aw + 2 * bw + ql + kvl
        win = jnp.concatenate([wi, wi[:, o_kr:o_kr + C_ROPE][:, perm]], axis=1).astype(BF16)
        wq = w_uq[li].reshape(ql, nh, C_NOPE + C_ROPE)
        wq_r = wq[:, :, C_NOPE:]
        wuq = jnp.concatenate([wq[:, :, :C_NOPE].reshape(ql, -1), wq_r.reshape(ql, -1), wq_r[:, :, perm].reshape(ql, -1)],
                              axis=1).astype(BF16)
        wkv = w_ukv[li].reshape(kvl, nh, C_NOPE + C_V)
        wukv = jnp.concatenate([wkv[:, :, :C_NOPE].reshape(kvl, -1), wkv[:, :, C_NOPE:].reshape(kvl, -1)], axis=1).astype(BF16)
        sguw = sgu_w[li].transpose(1, 0, 2).reshape(CHUNK, A_GROUPS * CHUNK).astype(BF16)
        sgub = jnp.repeat(sgu_b[li].T, aw // A_GROUPS, axis=1)

        ya, xb, gb, q, k, v = _in_call(xall, mod, row2(norm1_g[li]), win, sguw, sgub, row2(q_norm_g[li]), wuq,
                                       row2(kv_norm_g[li]), wukv, cos_t, sin_t, n_lat_tiles=nlt, nh=nh)

        lru = lambda dd: (conv_w[li], row2(conv_b[li]), jax.scipy.linalg.block_diag(*lru_wa[li, dd]).astype(BF16),
                          row2(lru_ba[li, dd]), jax.scipy.linalg.block_diag(*lru_wx[li, dd]).astype(BF16),
                          row2(lru_bx[li, dd]), row2(lru_lambda[li, dd]))
        hf = _lru_call(xb, None, None, *lru(0), reverse=False)
        yb = _lru_call(xb, hf, gb, *lru(1), reverse=True)

        yc = _attn_call(q, k, v, None, q_row0=0, nq_rows=t, kv_row0=0, kv_len=n, tq=Q_TILE, tk=KV_CHUNK, n_out=n)
        if not last:
            yc = _attn_call(q, k, v, yc, q_row0=t, nq_rows=t_ctx, kv_row0=t, kv_len=t_ctx, tq=t_ctx, tk=t_ctx, n_out=n)

        x1, h2, probs = _out_call(xall, ya, yb, yc, w_out[li].astype(BF16), mod, row2(norm2_g[li]), w_router[li],
                                  n_tiles=nlt if last else nt, n_lat_tiles=nlt)
        lp = {'w_gate': w_gate[li], 'w_up': w_up[li], 'w_down': w_down[li]}
        probs_t = probs.T
        if last:
            out = _moe(h2, probs_t[:, :t], probs[:t], x1, mod, row2(final_norm_g), None, lp, t=t, tile0=0,
                       n_lat_tiles=nlt, n_out=t)
        else:
            xnew = _moe(h2, probs_t[:, :t], probs[:t], x1, mod, None, None, lp, t=t, tile0=0, n_lat_tiles=nlt, n_out=n)
            xall = _moe(h2, probs_t[:, t:], probs[t:], x1, mod, None, xnew, lp, t=t_ctx, tile0=nlt, n_lat_tiles=nlt,
                        n_out=n)
    return out[None]
```

```python
import functools

import jax
import jax.numpy as jnp
import jax.scipy.linalg
from jax import lax
from jax.experimental import pallas as pl
from jax.experimental.pallas import tpu as pltpu

F32 = jnp.float32
BF16 = jnp.bfloat16
HIGHEST = lax.Precision.HIGHEST

GRID_W = 64
CHUNK = 128
EPS = 1e-6
A_GROUPS = 4
CONV_W = 4
LRU_C = 8.0
C_NOPE = 128
C_ROPE = 64
C_V = 128
ROPE_BASE = 10000.0
EC_CAPACITY = 2
LOG2E = 1.4426950408889634
SMALLEST_NORMAL = 1.1754943508222875e-38
BISECT_STEPS = 48

TOKEN_TILE = 256
LANES = 128
Q_TILE = 1024
Q_SUBTILE = 256
KV_CHUNK = 1280
DISPATCH_TILES = 4
COMBINE_TILES = 8
FFN_ROWS = 256
VMEM_LIMIT = 60 * 1024 * 1024


def _cparams(sem, vmem=None):
    return pltpu.CompilerParams(dimension_semantics=sem, vmem_limit_bytes=vmem)


def _rms(x):
    return x * lax.rsqrt(jnp.mean(x * x, axis=-1, keepdims=True) + EPS)


def _mod_kernel(cv_ref, w_ref, b_ref, o_ref):
    cv = cv_ref[...]
    s = cv * jax.nn.sigmoid(cv)
    o_ref[...] = jnp.dot(s, w_ref[...], precision=HIGHEST, preferred_element_type=F32) + b_ref[...]


def _mod_call(cv, w_mod, b_mod):
    depth, d, d6 = w_mod.shape
    tn = 1536
    return pl.pallas_call(
        _mod_kernel,
        grid=(depth, d6 // tn),
        in_specs=[
            pl.BlockSpec((8, d), lambda l, n: (0, 0)),
            pl.BlockSpec((None, d, tn), lambda l, n: (l, 0, n)),
            pl.BlockSpec((None, 1, tn), lambda l, n: (l, 0, n)),
        ],
        out_specs=pl.BlockSpec((None, 8, tn), lambda l, n: (l, 0, n)),
        out_shape=jax.ShapeDtypeStruct((depth, 8, d6), F32),
        compiler_params=_cparams(("arbitrary", "arbitrary")),
        name="modulation",
    )(cv, w_mod, b_mod.reshape(depth, 1, d6))


def _in_kernel(xl_ref, xc_ref, mod_ref, g1_ref, win_ref, sguw_ref, sgub_ref, qg_ref, wuq_ref, kvg_ref, wukv_ref, cos_ref, sin_ref,
               ya_ref, xb_ref, gb_ref, q_ref, k_ref, v_ref, *, n_lat_tiles, qscale):
    tm, d = xl_ref.shape
    is_ctx = pl.program_id(0) >= n_lat_tiles
    mod = mod_ref[...]
    m = jnp.where(is_ctx, mod[1:2], mod[0:1])
    shift, scale = m[:, 0:d], m[:, d:2 * d]
    h = _rms(jnp.where(is_ctx, xc_ref[...], xl_ref[...])) * g1_ref[...]
    h = h * (1.0 + scale) + shift
    z = jnp.dot(h.astype(BF16), win_ref[...], preferred_element_type=F32)

    aw = sgub_ref.shape[1]
    za = jax.nn.gelu(z[:, 0:2 * aw])
    u, val = za[:, :aw], za[:, aw:]
    vb = _rms(val).astype(BF16)
    gw = aw // A_GROUPS
    lane = lax.broadcasted_iota(jnp.int32, (CHUNK, aw), 1)
    for ci in range(tm // CHUNK):
        vc = vb[ci * CHUNK:(ci + 1) * CHUNK]
        vexp = jnp.concatenate(
            [jnp.where((lane >= g * gw) & (lane < (g + 1) * gw), vc, jnp.zeros_like(vc)) for g in range(A_GROUPS)], axis=0)
        mixed = jnp.dot(sguw_ref[...], vexp, preferred_element_type=F32) + sgub_ref[...]
        ya_ref[ci * CHUNK:(ci + 1) * CHUNK, :] = (u[ci * CHUNK:(ci + 1) * CHUNK] * mixed).astype(ya_ref.dtype)

    o = 2 * aw
    bw = xb_ref.shape[1]
    xb_ref[...] = z[:, o:o + bw]
    gb_ref[...] = jax.nn.gelu(z[:, o + bw:o + 2 * bw])
    o += 2 * bw

    ql = qg_ref.shape[1]
    kvl = kvg_ref.shape[1]
    nh = q_ref.shape[0]
    cos_t, sin_t = cos_ref[...], sin_ref[...]
    cqn = _rms(z[:, o:o + ql]) * qg_ref[...]
    qf = jnp.dot(cqn.astype(BF16), wuq_ref[...], preferred_element_type=F32)
    nn, nr = nh * C_NOPE, nh * C_ROPE
    qr = qf[:, nn:nn + nr] * cos_t + qf[:, nn + nr:nn + 2 * nr] * sin_t
    o += ql
    ckvn = _rms(z[:, o:o + kvl]) * kvg_ref[...]
    kvf = jnp.dot(ckvn.astype(BF16), wukv_ref[...], preferred_element_type=F32)
    o += kvl
    kr = z[:, o:o + C_ROPE] * cos_t[:, :C_ROPE] + z[:, o + C_ROPE:o + 2 * C_ROPE] * sin_t[:, :C_ROPE]
    krb = kr.astype(k_ref.dtype)
    for hh in range(nh):
        q_ref[hh, :, 0:C_NOPE] = (qf[:, hh * C_NOPE:(hh + 1) * C_NOPE] * qscale).astype(q_ref.dtype)
        q_ref[hh, :, C_NOPE:C_NOPE + C_ROPE] = (qr[:, hh * C_ROPE:(hh + 1) * C_ROPE] * qscale).astype(q_ref.dtype)
        k_ref[hh, :, 0:C_NOPE] = kvf[:, hh * C_NOPE:(hh + 1) * C_NOPE].astype(k_ref.dtype)
        k_ref[hh, :, C_NOPE:C_NOPE + C_ROPE] = krb
        v_ref[hh] = kvf[:, nn + hh * C_V:nn + (hh + 1) * C_V].astype(v_ref.dtype)


def _in_call(xl, xc, mod, g1, win, sguw, sgub, qg, wuq, kvg, wukv, cos_t, sin_t, *, nh):
    d = xl.shape[1]
    tm = TOKEN_TILE
    n_lat_tiles = xl.shape[0] // tm
    n = xl.shape[0] + xc.shape[0]
    aw = sgub.shape[1]
    bw = aw
    dk = C_NOPE + C_ROPE
    full = lambda a: pl.BlockSpec(a.shape, lambda i: (0,) * a.ndim)
    row = lambda w: pl.BlockSpec((tm, w), lambda i: (i, 0))
    kern = functools.partial(_in_kernel, n_lat_tiles=n_lat_tiles, qscale=float(dk ** -0.5 * LOG2E))
    return pl.pallas_call(
        kern,
        grid=(n // tm,),
        in_specs=[pl.BlockSpec((tm, d), lambda i: (jnp.minimum(i, n_lat_tiles - 1), 0)), full(xc),
                  full(mod), full(g1), full(win), full(sguw), full(sgub), full(qg), full(wuq), full(kvg), full(wukv),
                  row(cos_t.shape[1]), row(sin_t.shape[1])],
        out_specs=[row(aw), row(bw), row(bw),
                   pl.BlockSpec((nh, tm, dk), lambda i: (0, i, 0)),
                   pl.BlockSpec((nh, tm, dk), lambda i: (0, i, 0)),
                   pl.BlockSpec((nh, tm, C_V), lambda i: (0, i, 0))],
        out_shape=[jax.ShapeDtypeStruct((n, aw), BF16), jax.ShapeDtypeStruct((n, bw), F32), jax.ShapeDtypeStruct((n, bw), F32),
                   jax.ShapeDtypeStruct((nh, n, dk), BF16), jax.ShapeDtypeStruct((nh, n, dk), BF16),
                   jax.ShapeDtypeStruct((nh, n, C_V), BF16)],
        compiler_params=_cparams(("arbitrary",), VMEM_LIMIT),
        name="input_proj",
    )(xl, xc, mod, g1, win, sguw, sgub, qg, wuq, kvg, wukv, cos_t, sin_t)


def _tile_scan(a, b, reverse):
    n = a.shape[0]
    rows = lax.broadcasted_iota(jnp.int32, a.shape, 0)
    k = 1
    while k < n:
        sh = n - k if reverse else k
        valid = (rows < n - k) if reverse else (rows >= k)
        a_s = pltpu.roll(a, sh, 0)
        b_s = pltpu.roll(b, sh, 0)
        b = jnp.where(valid, a * b_s, 0.0) + b
        a = jnp.where(valid, a * a_s, a)
        k *= 2
    return a, b


def _softplus(x):
    return jnp.maximum(x, 0.0) + jnp.log(1.0 + jnp.exp(-jnp.abs(x)))


def _lru_kernel(*refs, reverse, nt):
    if reverse:
        xp_ref, x_ref, xn_ref, hf_ref, gb_ref, cw_ref, cb_ref, wa_ref, ba_ref, wx_ref, bx_ref, lam_ref, o_ref, carry_ref = refs
    else:
        xp_ref, x_ref, xn_ref, cw_ref, cb_ref, wa_ref, ba_ref, wx_ref, bx_ref, lam_ref, o_ref, carry_ref = refs
    tm = x_ref.shape[0]
    s = pl.program_id(0)
    tile = _lru_tile(s, nt, reverse)

    @pl.when(s == 0)
    def _():
        carry_ref[...] = jnp.zeros_like(carry_ref)

    prev_ok = jnp.logical_and(tile != 0, tile != nt - 1)
    next_ok = tile < nt - 2
    xp = jnp.where(prev_ok, xp_ref[...], 0.0)
    xn = jnp.where(next_ok, xn_ref[...], 0.0)
    ext = jnp.concatenate([xp, x_ref[...], xn], axis=0)
    cw = cw_ref[...]
    conv = cb_ref[...]
    for kk in range(CONV_W):
        st = 8 + kk - CONV_W // 2
        conv = conv + ext[st:st + tm] * cw[kk:kk + 1]

    cb16 = conv.astype(BF16)
    r = jax.nn.sigmoid(jnp.dot(cb16, wa_ref[...], preferred_element_type=F32) + ba_ref[...])
    gate = jax.nn.sigmoid(jnp.dot(cb16, wx_ref[...], preferred_element_type=F32) + bx_ref[...])
    log_a = (-LRU_C) * r * _softplus(-lam_ref[...])
    a = jnp.exp(log_a)
    mult = jnp.sqrt(jnp.maximum(-jnp.tanh(log_a) * (a * a + 1.0), 0.0))
    b = mult * (gate * conv)
    acum, bcum = _tile_scan(a, b, reverse)
    h = acum * carry_ref[...] + bcum
    carry_ref[...] = h[0:1] if reverse else h[tm - 1:tm]
    if reverse:
        o_ref[...] = ((hf_ref[...] + h) * gb_ref[...]).astype(o_ref.dtype)
    else:
        o_ref[...] = h


def _lru_tile(s, nt, reverse):
    if reverse:
        return jnp.where(s == 0, nt - 1, nt - 1 - s)
    return jnp.where(s == 0, nt - 1, s - 1)


def _lru_call(xb, hf, gb, cw, cb, wa, ba, wx, bx, lam, *, reverse):
    n, w = xb.shape
    tm = TOKEN_TILE
    nt = n // tm
    r8 = tm // 8
    tile = lambda s: _lru_tile(s, nt, reverse)
    row = pl.BlockSpec((tm, w), lambda s: (tile(s), 0))
    prev = pl.BlockSpec((8, w), lambda s: (jnp.maximum(tile(s) * r8 - 1, 0), 0))
    nxt = pl.BlockSpec((8, w), lambda s: (jnp.minimum((tile(s) + 1) * r8, n // 8 - 1), 0))
    full = lambda a: pl.BlockSpec(a.shape, lambda s: (0,) * a.ndim)
    params = [cw, cb, wa, ba, wx, bx, lam]
    ins = [xb, xb, xb] + ([hf, gb] if reverse else []) + params
    specs = [prev, row, nxt] + ([row, row] if reverse else []) + [full(p) for p in params]
    return pl.pallas_call(
        functools.partial(_lru_kernel, reverse=reverse, nt=nt),
        grid=(nt,),
        in_specs=specs,
        out_specs=row,
        out_shape=jax.ShapeDtypeStruct((n, w), BF16 if reverse else F32),
        scratch_shapes=[pltpu.VMEM((1, w), F32)],
        compiler_params=_cparams(("arbitrary",)),
        name="lru_bwd" if reverse else "lru_fwd",
    )(*ins)


def _attn_kernel(*refs, tk, nk, ts):
    q_ref, k_ref, v_ref, o_ref = refs[:4]
    scratch = refs[4:]
    nsub = q_ref.shape[0] // ts
    m_refs, l_refs, acc_refs = scratch[:nsub], scratch[nsub:2 * nsub], scratch[2 * nsub:]
    lanes = m_refs[0].shape[1]
    nblk = tk // lanes
    for u in range(nsub):
        m_refs[u][...] = jnp.full(m_refs[u].shape, -jnp.inf, F32)
        l_refs[u][...] = jnp.zeros(l_refs[u].shape, F32)
        acc_refs[u][...] = jnp.zeros(acc_refs[u].shape, F32)

    def body(c, carry):
        off = pl.multiple_of(c * tk, tk)
        k = k_ref[pl.ds(off, tk), :]
        v = v_ref[pl.ds(off, tk), :]
        for u in range(nsub):
            q = q_ref[u * ts:(u + 1) * ts, :]
            s = lax.dot_general(q, k, (((1,), (1,)), ((), ())), preferred_element_type=F32)
            m_prev = m_refs[u][...]
            m_new = jnp.maximum(m_prev, jnp.max(s, axis=-1, keepdims=True))
            alpha = jnp.exp2(m_prev - m_new)
            pb = [jnp.exp2(s[:, b * lanes:(b + 1) * lanes] - m_new) for b in range(nblk)]
            psum = pb[0]
            for b in range(1, nblk):
                psum = psum + pb[b]
            l_refs[u][...] = alpha * l_refs[u][...] + psum
            p = jnp.concatenate(pb, axis=1).astype(v.dtype)
            acc_refs[u][...] = alpha * acc_refs[u][...] + jnp.dot(p, v, preferred_element_type=F32)
            m_refs[u][...] = m_new
        return carry

    lax.fori_loop(0, nk, body, 0)
    for u in range(nsub):
        l = jnp.sum(l_refs[u][...], axis=-1, keepdims=True)
        o_ref[u * ts:(u + 1) * ts, :] = (acc_refs[u][...] / l).astype(o_ref.dtype)


def _attn_call(q, k, v, *, q_row0, nq_rows, kv_row0, kv_len, tq, tk):
    nh, _, dk = q.shape
    dv = v.shape[2]
    ts = min(tq, Q_SUBTILE)
    assert q_row0 % tq == 0 and nq_rows % tq == 0 and kv_row0 % kv_len == 0 and kv_len % tk == 0 and tq % ts == 0
    assert dv == LANES and tk % LANES == 0
    nsub = tq // ts
    qb0, kvb = q_row0 // tq, kv_row0 // kv_len
    in_specs = [pl.BlockSpec((None, tq, dk), lambda h, i: (h, qb0 + i, 0)),
                pl.BlockSpec((None, kv_len, dk), lambda h, i: (h, kvb, 0)),
                pl.BlockSpec((None, kv_len, dv), lambda h, i: (h, kvb, 0))]
    return pl.pallas_call(
        functools.partial(_attn_kernel, tk=tk, nk=kv_len // tk, ts=ts),
        grid=(nh, nq_rows // tq),
        in_specs=in_specs,
        out_specs=pl.BlockSpec((tq, dv), lambda h, i: (i, h)),
        out_shape=jax.ShapeDtypeStruct((nq_rows, nh * dv), BF16),
        scratch_shapes=[pltpu.VMEM((ts, LANES), F32)] * (3 * nsub),
        compiler_params=_cparams(("arbitrary", "arbitrary"), VMEM_LIMIT),
        name="attention",
    )(q, k, v)


def _out_kernel(xl_ref, xc_ref, ya_ref, yb_ref, ycl_ref, ycc_ref, wout_ref, mod_ref, g2_ref, wr_ref, x1_ref, h2_ref, p_ref, *,
                n_lat_tiles):
    d = xl_ref.shape[1]
    is_ctx = pl.program_id(0) >= n_lat_tiles
    x = jnp.where(is_ctx, xc_ref[...], xl_ref[...])
    yc = jnp.where(is_ctx, ycc_ref[...], ycl_ref[...])
    mod = mod_ref[...]
    m = jnp.where(is_ctx, mod[1:2], mod[0:1])
    gate1, shift2, scale2 = m[:, 2 * d:3 * d], m[:, 3 * d:4 * d], m[:, 4 * d:5 * d]
    mix = jnp.concatenate([ya_ref[...], yb_ref[...], yc], axis=-1)
    x1 = x + gate1 * jnp.dot(mix, wout_ref[...], preferred_element_type=F32)
    x1_ref[...] = x1
    h2 = _rms(x1) * g2_ref[...]
    h2 = h2 * (1.0 + scale2) + shift2
    h2_ref[...] = h2.astype(h2_ref.dtype)
    logits = jnp.dot(h2, wr_ref[...], precision=HIGHEST, preferred_element_type=F32)
    ex = jnp.exp(logits - jnp.max(logits, axis=-1, keepdims=True))
    p_ref[...] = ex / jnp.sum(ex, axis=-1, keepdims=True)


def _out_call(xl, xc, ya, yb, ycl, ycc, wout, mod, g2, wr, *, n_tiles):
    d = xl.shape[1]
    tm = TOKEN_TILE
    n_lat_tiles = xl.shape[0] // tm
    n = n_tiles * tm
    lat = lambda w: pl.BlockSpec((tm, w), lambda i: (jnp.minimum(i, n_lat_tiles - 1), 0))
    ne = wr.shape[1]
    full = lambda a: pl.BlockSpec(a.shape, lambda i: (0,) * a.ndim)
    row = lambda w: pl.BlockSpec((tm, w), lambda i: (i, 0))
    return pl.pallas_call(
        functools.partial(_out_kernel, n_lat_tiles=n_lat_tiles),
        grid=(n_tiles,),
        in_specs=[lat(d), full(xc), row(ya.shape[1]), row(yb.shape[1]), lat(ycl.shape[1]), full(ycc),
                  full(wout), full(mod), full(g2), full(wr)],
        out_specs=[row(d), row(d), row(ne)],
        out_shape=[jax.ShapeDtypeStruct((n, d), F32), jax.ShapeDtypeStruct((n, d), BF16), jax.ShapeDtypeStruct((n, ne), F32)],
        compiler_params=_cparams(("arbitrary",), VMEM_LIMIT),
        name="output_proj",
    )(xl, xc, ya, yb, ycl, ycc, wout, mod, g2, wr)


def _cumsum_lanes(x):
    n = x.shape[1]
    lane = lax.broadcasted_iota(jnp.int32, x.shape, 1)
    k = 1
    while k < n:
        x = x + jnp.where(lane >= k, pltpu.roll(x, k, 1), 0)
        k *= 2
    return x


def _count(mask):
    return jnp.sum(jnp.where(mask, 1.0, 0.0), axis=1, keepdims=True).astype(jnp.int32)


def _select_kernel(p_ref, pos_ref, excl_ref, *, cap):
    p = p_ref[...]

    def step(i, bounds):
        lo, hi = bounds
        lo_pos = jnp.maximum(lo, SMALLEST_NORMAL)
        mid = jnp.where(hi > 2.0 * lo_pos, jnp.sqrt(lo_pos) * jnp.sqrt(hi), lo + 0.5 * (hi - lo))
        mid = jnp.clip(mid, lo, hi)
        ok = _count(p >= mid) >= cap
        return jnp.where(ok, mid, lo), jnp.where(ok, hi, mid)

    rows = (p.shape[0], 1)
    lo, hi = lax.fori_loop(0, BISECT_STEPS, step, (jnp.zeros(rows, F32), jnp.full(rows, 2.0, F32)))
    gt = p >= hi
    eq = jnp.logical_and(p >= lo, p < hi)
    need = cap - _count(gt)
    eqi = jnp.where(eq, 1, 0)
    eq_rank = _cumsum_lanes(eqi) - eqi
    sel = jnp.logical_or(gt, jnp.logical_and(eq, eq_rank < need))
    seli = jnp.where(sel, 1, 0)
    excl = _cumsum_lanes(seli) - seli
    excl_ref[...] = excl
    pos_ref[...] = jnp.where(sel, excl, -1)


def _select_call(p_t, cap):
    ne, t = p_t.shape
    return pl.pallas_call(
        functools.partial(_select_kernel, cap=cap),
        out_shape=[jax.ShapeDtypeStruct((ne, t), jnp.int32), jax.ShapeDtypeStruct((ne, t), jnp.int32)],
        compiler_params=pltpu.CompilerParams(vmem_limit_bytes=VMEM_LIMIT),
        name="expert_select",
    )(p_t)


def _moe_ffn_kernel(c0_ref, pos_ref, h_ref, wg_ref, wu_ref, wd_ref, y_ref, xs_ref, *, nb, nsub, win, cap, rows):
    e = pl.program_id(0)
    j = pl.program_id(1)
    tm = TOKEN_TILE

    @pl.when(j == 0)
    def _():
        xs_ref[...] = jnp.zeros_like(xs_ref)

    slot = lax.broadcasted_iota(jnp.int32, (win, tm), 0)
    for u in range(nsub):
        c0 = c0_ref[e * nb + j * nsub + u]
        w0 = pl.multiple_of((c0 // 8) * 8, 8)
        rel = pos_ref[:, u * tm:(u + 1) * tm] - w0
        onehot = jnp.where(slot == rel, 1.0, 0.0).astype(BF16)
        xs_ref[pl.ds(w0, win), :] += jnp.dot(onehot, h_ref[u * tm:(u + 1) * tm, :], preferred_element_type=F32)

    @pl.when(j == pl.num_programs(1) - 1)
    def _():
        wg = wg_ref[...].astype(BF16)
        wu = wu_ref[...].astype(BF16)
        wd = wd_ref[...].astype(BF16)
        for ci in range(cap // rows):
            xc = xs_ref[ci * rows:(ci + 1) * rows, :].astype(BF16)
            hg = jnp.dot(xc, wg, preferred_element_type=F32)
            hu = jnp.dot(xc, wu, preferred_element_type=F32)
            hid = (hg * jax.nn.sigmoid(hg) * hu).astype(BF16)
            y_ref[ci * rows:(ci + 1) * rows, :] = jnp.dot(hid, wd, preferred_element_type=F32).astype(y_ref.dtype)


def _moe_ffn_call(c0, pos_t, h2, wg, wu, wd, *, cap, tile0):
    ne, t = pos_t.shape
    tm = TOKEN_TILE
    nb = t // tm
    nsub = min(nb, DISPATCH_TILES)
    gt = nsub * tm
    assert nb % nsub == 0 and tile0 % nsub == 0
    d = h2.shape[1]
    ff = wg.shape[2]
    win = min(tm, cap) + 16
    rows = min(cap, FFN_ROWS)
    return pl.pallas_call(
        functools.partial(_moe_ffn_kernel, nb=nb, nsub=nsub, win=win, cap=cap, rows=rows),
        grid_spec=pltpu.PrefetchScalarGridSpec(
            num_scalar_prefetch=1,
            grid=(ne, nb // nsub),
            in_specs=[pl.BlockSpec((None, 1, gt), lambda e, j, c: (e, 0, j)),
                      pl.BlockSpec((gt, d), lambda e, j, c: (tile0 // nsub + j, 0)),
                      pl.BlockSpec((None, d, ff), lambda e, j, c: (e, 0, 0)),
                      pl.BlockSpec((None, d, ff), lambda e, j, c: (e, 0, 0)),
                      pl.BlockSpec((None, ff, d), lambda e, j, c: (e, 0, 0))],
            out_specs=pl.BlockSpec((None, cap, d), lambda e, j, c: (e, 0, 0)),
            scratch_shapes=[pltpu.VMEM((cap + win, d), F32)]),
        out_shape=jax.ShapeDtypeStruct((ne, cap, d), BF16),
        compiler_params=_cparams(("arbitrary", "arbitrary"), VMEM_LIMIT),
        name="moe_ffn",
    )(c0, pos_t.reshape(ne, 1, t), h2, wg, wu, wd)


def _combine_kernel(*refs, nb, nsub, kwin, cap, mod_row, final):
    if final:
        c0_ref, y_ref, pos_ref, p_ref, x1_ref, mod_ref, fg_ref, o_ref = refs
    else:
        c0_ref, y_ref, pos_ref, p_ref, x1_ref, mod_ref, o_ref = refs
    g = pl.program_id(0)
    e = pl.program_id(1)
    tm = TOKEN_TILE
    d = x1_ref.shape[1]
    gate2 = mod_ref[mod_row:mod_row + 1, 5 * d:6 * d]

    @pl.when(e == 0)
    def _():
        o_ref[...] = x1_ref[...]

    slot = lax.broadcasted_iota(jnp.int32, (tm, kwin), 1)
    col = lax.broadcasted_iota(jnp.int32, (tm, pos_ref.shape[1]), 1) == e
    for u in range(nsub):
        c0 = c0_ref[e * nb + g * nsub + u]
        w0 = pl.multiple_of(jnp.minimum((c0 // 16) * 16, cap - kwin), 16)
        rs = slice(u * tm, (u + 1) * tm)
        slot_of = jnp.sum(jnp.where(col, pos_ref[rs, :].astype(F32), 0.0), axis=1, keepdims=True).astype(jnp.int32)
        gate = jnp.sum(jnp.where(col, p_ref[rs, :], 0.0), axis=1, keepdims=True)
        onehot = jnp.where(slot == slot_of - w0, 1.0, 0.0).astype(BF16)
        r = jnp.dot(onehot, y_ref[pl.ds(w0, kwin), :], preferred_element_type=F32)
        o_ref[rs, :] += gate * (gate2 * r)

    if final:
        @pl.when(e == pl.num_programs(1) - 1)
        def _():
            o_ref[...] = _rms(o_ref[...]) * fg_ref[...]


def _combine_call(c0, y, pos, probs, x1, mod, fg, *, cap, tile0, mod_row):
    ne = y.shape[0]
    t = pos.shape[0]
    tm = TOKEN_TILE
    nb = t // tm
    nsub = min(nb, COMBINE_TILES)
    gt = nsub * tm
    assert nb % nsub == 0 and tile0 % nsub == 0
    d = x1.shape[1]
    kwin = min(2 * tm, cap)
    final = fg is not None
    full = lambda a: pl.BlockSpec(a.shape, lambda g, e, c: (0,) * a.ndim)
    in_specs = [pl.BlockSpec((None, cap, d), lambda g, e, c: (e, 0, 0)),
                pl.BlockSpec((gt, ne), lambda g, e, c: (g, 0)),
                pl.BlockSpec((gt, ne), lambda g, e, c: (g, 0)),
                pl.BlockSpec((gt, d), lambda g, e, c: (tile0 // nsub + g, 0)),
                full(mod)]
    ins = [y, pos, probs, x1, mod]
    if final:
        in_specs.append(full(fg))
        ins.append(fg)
    return pl.pallas_call(
        functools.partial(_combine_kernel, nb=nb, nsub=nsub, kwin=kwin, cap=cap, mod_row=mod_row, final=final),
        grid_spec=pltpu.PrefetchScalarGridSpec(
            num_scalar_prefetch=1,
            grid=(nb // nsub, ne),
            in_specs=in_specs,
            out_specs=pl.BlockSpec((gt, d), lambda g, e, c: (g, 0))),
        out_shape=jax.ShapeDtypeStruct((t, d), F32),
        compiler_params=_cparams(("arbitrary", "arbitrary"), VMEM_LIMIT),
        name="moe_combine",
    )(c0, *ins)


def _rope_tables(t, t_ctx, nh):
    pos = jnp.arange(t)
    r = (pos // GRID_W).astype(F32)
    col = (pos % GRID_W).astype(F32)
    nf = C_ROPE // 4
    inv = ROPE_BASE ** (-jnp.arange(nf, dtype=F32) / nf)
    ar, ac = r[:, None] * inv, col[:, None] * inv
    cos_t = jnp.concatenate([jnp.cos(ar), jnp.cos(ar), jnp.cos(ac), jnp.cos(ac)], axis=-1)
    sin_t = jnp.concatenate([-jnp.sin(ar), jnp.sin(ar), -jnp.sin(ac), jnp.sin(ac)], axis=-1)
    cos_t = jnp.concatenate([cos_t, jnp.ones((t_ctx, C_ROPE), F32)], axis=0)
    sin_t = jnp.concatenate([sin_t, jnp.zeros((t_ctx, C_ROPE), F32)], axis=0)
    return jnp.tile(cos_t, (1, nh)), jnp.tile(sin_t, (1, nh))


def _swap_perm():
    nf = C_ROPE // 4
    idx = jnp.arange(C_ROPE).reshape(2, 2, nf)
    return idx[:, ::-1, :].reshape(-1)


def _moe(h2, probs_t, probs, x1, mod, fg, lp, *, tile0, mod_row):
    ne, t = probs_t.shape
    cap = EC_CAPACITY * t // ne
    pos_t, excl = _select_call(probs_t, cap)
    c0 = excl[:, ::TOKEN_TILE].reshape(-1)
    y = _moe_ffn_call(c0, pos_t, h2, lp['w_gate'], lp['w_up'], lp['w_down'], cap=cap, tile0=tile0)
    return _combine_call(c0, y, pos_t.T, probs, x1, mod, fg, cap=cap, tile0=tile0, mod_row=mod_row)


def kernel(x, c, ctx, c_ctx, norm1_g, w_mod, b_mod, w_in, sgu_w, sgu_b, conv_w, conv_b, lru_wa, lru_ba, lru_wx, lru_bx, lru_lambda, q_norm_g, w_uq, kv_norm_g, w_ukv, w_out, norm2_g, w_router, w_gate, w_up, w_down, final_norm_g):
    bsz, t, d = x.shape
    t_ctx = ctx.shape[1]
    depth = w_mod.shape[0]
    tm = TOKEN_TILE
    assert bsz == 1 and t_ctx == tm and t % Q_TILE == 0 and t % GRID_W == 0
    n = t + t_ctx
    nlt = t // tm
    nt = n // tm
    ql, kvl = q_norm_g.shape[1], kv_norm_g.shape[1]
    nh = w_uq.shape[2] // (C_NOPE + C_ROPE)
    aw = bw = (w_in.shape[2] - ql - kvl - C_ROPE) // 4
    assert (n % KV_CHUNK) == 0

    xl, xc = x[0], ctx[0]
    cv = jnp.zeros((8, d), F32).at[0].set(c[0]).at[1].set(c_ctx)
    mods = _mod_call(cv, w_mod, b_mod)[:, :2]
    cos_t, sin_t = _rope_tables(t, t_ctx, nh)
    perm = _swap_perm()
    row2 = lambda a: a.reshape(1, -1)

    for li in range(depth):
        last = li == depth - 1
        mod = mods[li]
        wi = w_in[li]
        o_kr = 2 * aw + 2 * bw + ql + kvl
        win = jnp.concatenate([wi, wi[:, o_kr:o_kr + C_ROPE][:, perm]], axis=1).astype(BF16)
        wq = w_uq[li].reshape(ql, nh, C_NOPE + C_ROPE)
        wq_r = wq[:, :, C_NOPE:]
        wuq = jnp.concatenate([wq[:, :, :C_NOPE].reshape(ql, -1), wq_r.reshape(ql, -1), wq_r[:, :, perm].reshape(ql, -1)],
                              axis=1).astype(BF16)
        wkv = w_ukv[li].reshape(kvl, nh, C_NOPE + C_V)
        wukv = jnp.concatenate([wkv[:, :, :C_NOPE].reshape(kvl, -1), wkv[:, :, C_NOPE:].reshape(kvl, -1)], axis=1).astype(BF16)
        sguw = sgu_w[li].transpose(1, 0, 2).reshape(CHUNK, A_GROUPS * CHUNK).astype(BF16)
        sgub = jnp.repeat(sgu_b[li].T, aw // A_GROUPS, axis=1)

        ya, xb, gb, q, k, v = _in_call(xl, xc, mod, row2(norm1_g[li]), win, sguw, sgub, row2(q_norm_g[li]), wuq,
                                       row2(kv_norm_g[li]), wukv, cos_t, sin_t, nh=nh)

        lru = lambda dd: (conv_w[li], row2(conv_b[li]), jax.scipy.linalg.block_diag(*lru_wa[li, dd]).astype(BF16),
                          row2(lru_ba[li, dd]), jax.scipy.linalg.block_diag(*lru_wx[li, dd]).astype(BF16),
                          row2(lru_bx[li, dd]), row2(lru_lambda[li, dd]))
        hf = _lru_call(xb, None, None, *lru(0), reverse=False)
        yb = _lru_call(xb, hf, gb, *lru(1), reverse=True)

        ycl = _attn_call(q, k, v, q_row0=0, nq_rows=t, kv_row0=0, kv_len=n, tq=Q_TILE, tk=KV_CHUNK)
        ycc = ycl[:t_ctx] if last else _attn_call(q, k, v, q_row0=t, nq_rows=t_ctx, kv_row0=t, kv_len=t_ctx, tq=t_ctx, tk=t_ctx)

        x1, h2, probs = _out_call(xl, xc, ya, yb, ycl, ycc, w_out[li].astype(BF16), mod, row2(norm2_g[li]), w_router[li],
                                  n_tiles=nlt if last else nt)
        lp = {'w_gate': w_gate[li], 'w_up': w_up[li], 'w_down': w_down[li]}
        probs_t = probs.T
        xl = _moe(h2, probs_t[:, :t], probs[:t], x1, mod, row2(final_norm_g) if last else None, lp, tile0=0, mod_row=0)
        if not last:
            xc = _moe(h2, probs_t[:, t:], probs[t:], x1, mod, None, lp, tile0=nlt, mod_row=1)
    return xl[None]
```

```python
import functools

import jax
import jax.numpy as jnp
import jax.scipy.linalg
from jax import lax
from jax.experimental import pallas as pl
from jax.experimental.pallas import tpu as pltpu

F32 = jnp.float32
BF16 = jnp.bfloat16
HIGHEST = lax.Precision.HIGHEST

GRID_W = 64
CHUNK = 128
EPS = 1e-6
A_GROUPS = 4
CONV_W = 4
LRU_C = 8.0
C_NOPE = 128
C_ROPE = 64
C_V = 128
ROPE_BASE = 10000.0
EC_CAPACITY = 2
LOG2E = 1.4426950408889634
SMALLEST_NORMAL = 1.1754943508222875e-38
BISECT_STEPS = 48

TOKEN_TILE = 256
LANES = 128
Q_TILE = 2048
Q_SUBTILE = 512
KV_CHUNK = 1280
KV_UNROLL = 2
DISPATCH_TILES = 4
COMBINE_TILES = 8
FFN_ROWS = 256
VMEM_LIMIT = 60 * 1024 * 1024


def _cparams(sem, vmem=None):
    return pltpu.CompilerParams(dimension_semantics=sem, vmem_limit_bytes=vmem)


def _rms(x):
    return x * lax.rsqrt(jnp.mean(x * x, axis=-1, keepdims=True) + EPS)


def _mod_kernel(cv_ref, w_ref, b_ref, o_ref):
    cv = cv_ref[...]
    s = cv * jax.nn.sigmoid(cv)
    o_ref[...] = jnp.dot(s, w_ref[...], precision=HIGHEST, preferred_element_type=F32) + b_ref[...]


def _mod_call(cv, w_mod, b_mod):
    depth, d, d6 = w_mod.shape
    tn = 1536
    return pl.pallas_call(
        _mod_kernel,
        grid=(depth, d6 // tn),
        in_specs=[
            pl.BlockSpec((8, d), lambda l, n: (0, 0)),
            pl.BlockSpec((None, d, tn), lambda l, n: (l, 0, n)),
            pl.BlockSpec((None, 1, tn), lambda l, n: (l, 0, n)),
        ],
        out_specs=pl.BlockSpec((None, 8, tn), lambda l, n: (l, 0, n)),
        out_shape=jax.ShapeDtypeStruct((depth, 8, d6), F32),
        compiler_params=_cparams(("arbitrary", "arbitrary")),
        name="modulation",
    )(cv, w_mod, b_mod.reshape(depth, 1, d6))


def _in_kernel(xl_ref, xc_ref, mod_ref, g1_ref, win_ref, sguw_ref, sgub_ref, qg_ref, wuq_ref, kvg_ref, wukv_ref, cos_ref, sin_ref,
               ya_ref, xb_ref, gb_ref, q_ref, kt_ref, v_ref, *, n_lat_tiles, qscale):
    tm, d = xl_ref.shape
    is_ctx = pl.program_id(0) >= n_lat_tiles
    mod = mod_ref[...]
    m = jnp.where(is_ctx, mod[1:2], mod[0:1])
    shift, scale = m[:, 0:d], m[:, d:2 * d]
    h = _rms(jnp.where(is_ctx, xc_ref[...], xl_ref[...])) * g1_ref[...]
    h = h * (1.0 + scale) + shift
    z = jnp.dot(h.astype(BF16), win_ref[...], preferred_element_type=F32)

    aw = sgub_ref.shape[1]
    za = jax.nn.gelu(z[:, 0:2 * aw])
    u, val = za[:, :aw], za[:, aw:]
    vb = _rms(val).astype(BF16)
    gw = aw // A_GROUPS
    lane = lax.broadcasted_iota(jnp.int32, (CHUNK, aw), 1)
    for ci in range(tm // CHUNK):
        vc = vb[ci * CHUNK:(ci + 1) * CHUNK]
        vexp = jnp.concatenate(
            [jnp.where((lane >= g * gw) & (lane < (g + 1) * gw), vc, jnp.zeros_like(vc)) for g in range(A_GROUPS)], axis=0)
        mixed = jnp.dot(sguw_ref[...], vexp, preferred_element_type=F32) + sgub_ref[...]
        ya_ref[ci * CHUNK:(ci + 1) * CHUNK, :] = (u[ci * CHUNK:(ci + 1) * CHUNK] * mixed).astype(ya_ref.dtype)

    o = 2 * aw
    bw = xb_ref.shape[1]
    xb_ref[...] = z[:, o:o + bw]
    gb_ref[...] = jax.nn.gelu(z[:, o + bw:o + 2 * bw])
    o += 2 * bw

    ql = qg_ref.shape[1]
    kvl = kvg_ref.shape[1]
    nh = q_ref.shape[0]
    cos_t, sin_t = cos_ref[...], sin_ref[...]
    cqn = _rms(z[:, o:o + ql]) * qg_ref[...]
    qf = jnp.dot(cqn.astype(BF16), wuq_ref[...], preferred_element_type=F32)
    nn, nr = nh * C_NOPE, nh * C_ROPE
    qr = qf[:, nn:nn + nr] * cos_t + qf[:, nn + nr:nn + 2 * nr] * sin_t
    o += ql
    ckvn = _rms(z[:, o:o + kvl]) * kvg_ref[...]
    kvf = jnp.dot(ckvn.astype(BF16), wukv_ref[...], preferred_element_type=F32)
    o += kvl
    kr = z[:, o:o + C_ROPE] * cos_t[:, :C_ROPE] + z[:, o + C_ROPE:o + 2 * C_ROPE] * sin_t[:, :C_ROPE]
    dk = C_NOPE + C_ROPE
    kr_pad = jnp.concatenate([kr, jnp.zeros((tm, 2 * LANES - dk), F32)], axis=1)
    for hh in range(nh):
        q_ref[hh, :, 0:C_NOPE] = (qf[:, hh * C_NOPE:(hh + 1) * C_NOPE] * qscale).astype(q_ref.dtype)
        q_ref[hh, :, C_NOPE:dk] = (qr[:, hh * C_ROPE:(hh + 1) * C_ROPE] * qscale).astype(q_ref.dtype)
        kh = jnp.concatenate([kvf[:, hh * C_NOPE:(hh + 1) * C_NOPE], kr_pad], axis=1)
        kt_ref[hh] = kh.T[:dk].astype(kt_ref.dtype)
        v_ref[hh] = kvf[:, nn + hh * C_V:nn + (hh + 1) * C_V].astype(v_ref.dtype)


def _in_call(xl, xc, mod, g1, win, sguw, sgub, qg, wuq, kvg, wukv, cos_t, sin_t, *, nh):
    d = xl.shape[1]
    tm = TOKEN_TILE
    n_lat_tiles = xl.shape[0] // tm
    n = xl.shape[0] + xc.shape[0]
    aw = sgub.shape[1]
    bw = aw
    dk = C_NOPE + C_ROPE
    full = lambda a: pl.BlockSpec(a.shape, lambda i: (0,) * a.ndim)
    row = lambda w: pl.BlockSpec((tm, w), lambda i: (i, 0))
    kern = functools.partial(_in_kernel, n_lat_tiles=n_lat_tiles, qscale=float(dk ** -0.5 * LOG2E))
    return pl.pallas_call(
        kern,
        grid=(n // tm,),
        in_specs=[pl.BlockSpec((tm, d), lambda i: (jnp.minimum(i, n_lat_tiles - 1), 0)), full(xc),
                  full(mod), full(g1), full(win), full(sguw), full(sgub), full(qg), full(wuq), full(kvg), full(wukv),
                  row(cos_t.shape[1]), row(sin_t.shape[1])],
        out_specs=[row(aw), row(bw), row(bw),
                   pl.BlockSpec((nh, tm, dk), lambda i: (0, i, 0)),
                   pl.BlockSpec((nh, dk, tm), lambda i: (0, 0, i)),
                   pl.BlockSpec((nh, tm, C_V), lambda i: (0, i, 0))],
        out_shape=[jax.ShapeDtypeStruct((n, aw), BF16), jax.ShapeDtypeStruct((n, bw), F32), jax.ShapeDtypeStruct((n, bw), F32),
                   jax.ShapeDtypeStruct((nh, n, dk), BF16), jax.ShapeDtypeStruct((nh, dk, n), BF16),
                   jax.ShapeDtypeStruct((nh, n, C_V), BF16)],
        compiler_params=_cparams(("arbitrary",), VMEM_LIMIT),
        name="input_proj",
    )(xl, xc, mod, g1, win, sguw, sgub, qg, wuq, kvg, wukv, cos_t, sin_t)


def _tile_scan(a, b, reverse):
    n = a.shape[0]
    rows = lax.broadcasted_iota(jnp.int32, a.shape, 0)
    k = 1
    while k < n:
        sh = n - k if reverse else k
        valid = (rows < n - k) if reverse else (rows >= k)
        a_s = pltpu.roll(a, sh, 0)
        b_s = pltpu.roll(b, sh, 0)
        b = jnp.where(valid, a * b_s, 0.0) + b
        a = jnp.where(valid, a * a_s, a)
        k *= 2
    return a, b


def _softplus(x):
    return jnp.maximum(x, 0.0) + jnp.log(1.0 + jnp.exp(-jnp.abs(x)))


def _lru_kernel(*refs, reverse, nt):
    if reverse:
        xp_ref, x_ref, xn_ref, hf_ref, gb_ref, cw_ref, cb_ref, wa_ref, ba_ref, wx_ref, bx_ref, lam_ref, o_ref, carry_ref = refs
    else:
        xp_ref, x_ref, xn_ref, cw_ref, cb_ref, wa_ref, ba_ref, wx_ref, bx_ref, lam_ref, o_ref, carry_ref = refs
    tm = x_ref.shape[0]
    s = pl.program_id(0)
    tile = _lru_tile(s, nt, reverse)

    @pl.when(s == 0)
    def _():
        carry_ref[...] = jnp.zeros_like(carry_ref)

    prev_ok = jnp.logical_and(tile != 0, tile != nt - 1)
    next_ok = tile < nt - 2
    xp = jnp.where(prev_ok, xp_ref[...], 0.0)
    xn = jnp.where(next_ok, xn_ref[...], 0.0)
    ext = jnp.concatenate([xp, x_ref[...], xn], axis=0)
    cw = cw_ref[...]
    conv = cb_ref[...]
    for kk in range(CONV_W):
        st = 8 + kk - CONV_W // 2
        conv = conv + ext[st:st + tm] * cw[kk:kk + 1]

    cb16 = conv.astype(BF16)
    r = jax.nn.sigmoid(jnp.dot(cb16, wa_ref[...], preferred_element_type=F32) + ba_ref[...])
    gate = jax.nn.sigmoid(jnp.dot(cb16, wx_ref[...], preferred_element_type=F32) + bx_ref[...])
    log_a = (-LRU_C) * r * _softplus(-lam_ref[...])
    a = jnp.exp(log_a)
    mult = jnp.sqrt(jnp.maximum(-jnp.tanh(log_a) * (a * a + 1.0), 0.0))
    b = mult * (gate * conv)
    acum, bcum = _tile_scan(a, b, reverse)
    h = acum * carry_ref[...] + bcum
    carry_ref[...] = h[0:1] if reverse else h[tm - 1:tm]
    if reverse:
        o_ref[...] = ((hf_ref[...] + h) * gb_ref[...]).astype(o_ref.dtype)
    else:
        o_ref[...] = h


def _lru_tile(s, nt, reverse):
    if reverse:
        return jnp.where(s == 0, nt - 1, nt - 1 - s)
    return jnp.where(s == 0, nt - 1, s - 1)


def _lru_call(xb, hf, gb, cw, cb, wa, ba, wx, bx, lam, *, reverse):
    n, w = xb.shape
    tm = TOKEN_TILE
    nt = n // tm
    r8 = tm // 8
    tile = lambda s: _lru_tile(s, nt, reverse)
    row = pl.BlockSpec((tm, w), lambda s: (tile(s), 0))
    prev = pl.BlockSpec((8, w), lambda s: (jnp.maximum(tile(s) * r8 - 1, 0), 0))
    nxt = pl.BlockSpec((8, w), lambda s: (jnp.minimum((tile(s) + 1) * r8, n // 8 - 1), 0))
    full = lambda a: pl.BlockSpec(a.shape, lambda s: (0,) * a.ndim)
    params = [cw, cb, wa, ba, wx, bx, lam]
    ins = [xb, xb, xb] + ([hf, gb] if reverse else []) + params
    specs = [prev, row, nxt] + ([row, row] if reverse else []) + [full(p) for p in params]
    return pl.pallas_call(
        functools.partial(_lru_kernel, reverse=reverse, nt=nt),
        grid=(nt,),
        in_specs=specs,
        out_specs=row,
        out_shape=jax.ShapeDtypeStruct((n, w), BF16 if reverse else F32),
        scratch_shapes=[pltpu.VMEM((1, w), F32)],
        compiler_params=_cparams(("arbitrary",)),
        name="lru_bwd" if reverse else "lru_fwd",
    )(*ins)


def _attn_kernel(*refs, tk, nk, ts):
    q_ref, kt_ref, v_ref, o_ref = refs[:4]
    scratch = refs[4:]
    nsub = q_ref.shape[0] // ts
    m_refs, l_refs, acc_refs = scratch[:nsub], scratch[nsub:2 * nsub], scratch[2 * nsub:]
    lanes = m_refs[0].shape[1]
    nblk = tk // lanes
    for u in range(nsub):
        m_refs[u][...] = jnp.full(m_refs[u].shape, -jnp.inf, F32)
        l_refs[u][...] = jnp.zeros(l_refs[u].shape, F32)
        acc_refs[u][...] = jnp.zeros(acc_refs[u].shape, F32)

    def body(c, carry):
        off = pl.multiple_of(c * tk, tk)
        kt = kt_ref[:, pl.ds(off, tk)]
        v = v_ref[pl.ds(off, tk), :]
        for u in range(nsub):
            q = q_ref[u * ts:(u + 1) * ts, :]
            s = jnp.dot(q, kt, preferred_element_type=F32)
            m_prev = m_refs[u][...]
            m_new = jnp.maximum(m_prev, jnp.max(s, axis=-1, keepdims=True))
            alpha = jnp.exp2(m_prev - m_new)
            pb = [jnp.exp2(s[:, b * lanes:(b + 1) * lanes] - m_new) for b in range(nblk)]
            psum = pb[0]
            for b in range(1, nblk):
                psum = psum + pb[b]
            l_refs[u][...] = alpha * l_refs[u][...] + psum
            p = jnp.concatenate(pb, axis=1).astype(v.dtype)
            acc_refs[u][...] = alpha * acc_refs[u][...] + jnp.dot(p, v, preferred_element_type=F32)
            m_refs[u][...] = m_new
        return carry

    lax.fori_loop(0, nk, body, 0, unroll=min(nk, KV_UNROLL))
    for u in range(nsub):
        l = jnp.sum(l_refs[u][...], axis=-1, keepdims=True)
        o_ref[u * ts:(u + 1) * ts, :] = (acc_refs[u][...] / l).astype(o_ref.dtype)


def _attn_call(q, kt, v, *, q_row0, nq_rows, kv_row0, kv_len, tq, tk):
    nh, _, dk = q.shape
    dv = v.shape[2]
    ts = min(tq, Q_SUBTILE)
    assert q_row0 % tq == 0 and nq_rows % tq == 0 and kv_row0 % kv_len == 0 and kv_len % tk == 0 and tq % ts == 0
    assert dv == LANES and tk % LANES == 0
    nsub = tq // ts
    qb0, kvb = q_row0 // tq, kv_row0 // kv_len
    in_specs = [pl.BlockSpec((None, tq, dk), lambda h, i: (h, qb0 + i, 0)),
                pl.BlockSpec((None, dk, kv_len), lambda h, i: (h, 0, kvb)),
                pl.BlockSpec((None, kv_len, dv), lambda h, i: (h, kvb, 0))]
    return pl.pallas_call(
        functools.partial(_attn_kernel, tk=tk, nk=kv_len // tk, ts=ts),
        grid=(nh, nq_rows // tq),
        in_specs=in_specs,
        out_specs=pl.BlockSpec((tq, dv), lambda h, i: (i, h)),
        out_shape=jax.ShapeDtypeStruct((nq_rows, nh * dv), BF16),
        scratch_shapes=[pltpu.VMEM((ts, LANES), F32)] * (3 * nsub),
        compiler_params=_cparams(("arbitrary", "arbitrary"), VMEM_LIMIT),
        name="attention",
    )(q, kt, v)


def _out_kernel(xl_ref, xc_ref, ya_ref, yb_ref, ycl_ref, ycc_ref, wout_ref, mod_ref, g2_ref, wr_ref, x1_ref, h2_ref, p_ref, *,
                n_lat_tiles):
    d = xl_ref.shape[1]
    is_ctx = pl.program_id(0) >= n_lat_tiles
    x = jnp.where(is_ctx, xc_ref[...], xl_ref[...])
    yc = jnp.where(is_ctx, ycc_ref[...], ycl_ref[...])
    mod = mod_ref[...]
    m = jnp.where(is_ctx, mod[1:2], mod[0:1])
    gate1, shift2, scale2 = m[:, 2 * d:3 * d], m[:, 3 * d:4 * d], m[:, 4 * d:5 * d]
    mix = jnp.concatenate([ya_ref[...], yb_ref[...], yc], axis=-1)
    x1 = x + gate1 * jnp.dot(mix, wout_ref[...], preferred_element_type=F32)
    x1_ref[...] = x1
    h2 = _rms(x1) * g2_ref[...]
    h2 = h2 * (1.0 + scale2) + shift2
    h2_ref[...] = h2.astype(h2_ref.dtype)
    logits = jnp.dot(h2, wr_ref[...], precision=HIGHEST, preferred_element_type=F32)
    ex = jnp.exp(logits - jnp.max(logits, axis=-1, keepdims=True))
    p_ref[...] = ex / jnp.sum(ex, axis=-1, keepdims=True)


def _out_call(xl, xc, ya, yb, ycl, ycc, wout, mod, g2, wr, *, n_tiles):
    d = xl.shape[1]
    tm = TOKEN_TILE
    n_lat_tiles = xl.shape[0] // tm
    n = n_tiles * tm
    lat = lambda w: pl.BlockSpec((tm, w), lambda i: (jnp.minimum(i, n_lat_tiles - 1), 0))
    ne = wr.shape[1]
    full = lambda a: pl.BlockSpec(a.shape, lambda i: (0,) * a.ndim)
    row = lambda w: pl.BlockSpec((tm, w), lambda i: (i, 0))
    return pl.pallas_call(
        functools.partial(_out_kernel, n_lat_tiles=n_lat_tiles),
        grid=(n_tiles,),
        in_specs=[lat(d), full(xc), row(ya.shape[1]), row(yb.shape[1]), lat(ycl.shape[1]), full(ycc),
                  full(wout), full(mod), full(g2), full(wr)],
        out_specs=[row(d), row(d), row(ne)],
        out_shape=[jax.ShapeDtypeStruct((n, d), F32), jax.ShapeDtypeStruct((n, d), BF16), jax.ShapeDtypeStruct((n, ne), F32)],
        compiler_params=_cparams(("arbitrary",), VMEM_LIMIT),
        name="output_proj",
    )(xl, xc, ya, yb, ycl, ycc, wout, mod, g2, wr)


def _cumsum_lanes(x):
    n = x.shape[1]
    lane = lax.broadcasted_iota(jnp.int32, x.shape, 1)
    k = 1
    while k < n:
        x = x + jnp.where(lane >= k, pltpu.roll(x, k, 1), 0)
        k *= 2
    return x


def _count(mask):
    return jnp.sum(jnp.where(mask, 1.0, 0.0), axis=1, keepdims=True).astype(jnp.int32)


def _select_kernel(p_ref, pos_ref, excl_ref, *, cap):
    p = p_ref[...]

    def step(i, bounds):
        lo, hi = bounds
        lo_pos = jnp.maximum(lo, SMALLEST_NORMAL)
        mid = jnp.where(hi > 2.0 * lo_pos, jnp.sqrt(lo_pos) * jnp.sqrt(hi), lo + 0.5 * (hi - lo))
        mid = jnp.clip(mid, lo, hi)
        ok = _count(p >= mid) >= cap
        return jnp.where(ok, mid, lo), jnp.where(ok, hi, mid)

    rows = (p.shape[0], 1)
    lo, hi = lax.fori_loop(0, BISECT_STEPS, step, (jnp.zeros(rows, F32), jnp.full(rows, 2.0, F32)))
    gt = p >= hi
    eq = jnp.logical_and(p >= lo, p < hi)
    need = cap - _count(gt)
    eqi = jnp.where(eq, 1, 0)
    eq_rank = _cumsum_lanes(eqi) - eqi
    sel = jnp.logical_or(gt, jnp.logical_and(eq, eq_rank < need))
    seli = jnp.where(sel, 1, 0)
    excl = _cumsum_lanes(seli) - seli
    excl_ref[...] = excl
    pos_ref[...] = jnp.where(sel, excl, -1)


def _select_call(p_t, cap):
    ne, t = p_t.shape
    return pl.pallas_call(
        functools.partial(_select_kernel, cap=cap),
        out_shape=[jax.ShapeDtypeStruct((ne, t), jnp.int32), jax.ShapeDtypeStruct((ne, t), jnp.int32)],
        compiler_params=pltpu.CompilerParams(vmem_limit_bytes=VMEM_LIMIT),
        name="expert_select",
    )(p_t)


def _moe_ffn_kernel(c0_ref, pos_ref, h_ref, wg_ref, wu_ref, wd_ref, y_ref, xs_ref, *, nb, nsub, win, cap, rows):
    e = pl.program_id(0)
    j = pl.program_id(1)
    tm = TOKEN_TILE

    @pl.when(j == 0)
    def _():
        xs_ref[...] = jnp.zeros_like(xs_ref)

    slot = lax.broadcasted_iota(jnp.int32, (win, tm), 0)
    for u in range(nsub):
        c0 = c0_ref[e * nb + j * nsub + u]
        w0 = pl.multiple_of((c0 // 8) * 8, 8)
        rel = pos_ref[:, u * tm:(u + 1) * tm] - w0
        onehot = jnp.where(slot == rel, 1.0, 0.0).astype(BF16)
        xs_ref[pl.ds(w0, win), :] += jnp.dot(onehot, h_ref[u * tm:(u + 1) * tm, :], preferred_element_type=F32)

    @pl.when(j == pl.num_programs(1) - 1)
    def _():
        wg = wg_ref[...].astype(BF16)
        wu = wu_ref[...].astype(BF16)
        wd = wd_ref[...].astype(BF16)
        for ci in range(cap // rows):
            xc = xs_ref[ci * rows:(ci + 1) * rows, :].astype(BF16)
            hg = jnp.dot(xc, wg, preferred_element_type=F32)
            hu = jnp.dot(xc, wu, preferred_element_type=F32)
            hid = (hg * jax.nn.sigmoid(hg) * hu).astype(BF16)
            y_ref[ci * rows:(ci + 1) * rows, :] = jnp.dot(hid, wd, preferred_element_type=F32).astype(y_ref.dtype)


def _moe_ffn_call(c0, pos_t, h2, wg, wu, wd, *, layer, cap, tile0):
    ne, t = pos_t.shape
    tm = TOKEN_TILE
    nb = t // tm
    nsub = min(nb, DISPATCH_TILES)
    gt = nsub * tm
    assert nb % nsub == 0 and tile0 % nsub == 0
    d = h2.shape[1]
    ff = wg.shape[3]
    win = min(tm, cap) + 16
    rows = min(cap, FFN_ROWS)
    return pl.pallas_call(
        functools.partial(_moe_ffn_kernel, nb=nb, nsub=nsub, win=win, cap=cap, rows=rows),
        grid_spec=pltpu.PrefetchScalarGridSpec(
            num_scalar_prefetch=1,
            grid=(ne, nb // nsub),
            in_specs=[pl.BlockSpec((None, 1, gt), lambda e, j, c: (e, 0, j)),
                      pl.BlockSpec((gt, d), lambda e, j, c: (tile0 // nsub + j, 0)),
                      pl.BlockSpec((None, None, d, ff), lambda e, j, c: (layer, e, 0, 0)),
                      pl.BlockSpec((None, None, d, ff), lambda e, j, c: (layer, e, 0, 0)),
                      pl.BlockSpec((None, None, ff, d), lambda e, j, c: (layer, e, 0, 0))],
            out_specs=pl.BlockSpec((None, cap, d), lambda e, j, c: (e, 0, 0)),
            scratch_shapes=[pltpu.VMEM((cap + win, d), F32)]),
        out_shape=jax.ShapeDtypeStruct((ne, cap, d), BF16),
        compiler_params=_cparams(("arbitrary", "arbitrary"), VMEM_LIMIT),
        name="moe_ffn",
    )(c0, pos_t.reshape(ne, 1, t), h2, wg, wu, wd)


def _combine_kernel(*refs, nb, nsub, kwin, cap, mod_row, final):
    if final:
        c0_ref, y_ref, pos_ref, p_ref, x1_ref, mod_ref, fg_ref, o_ref = refs
    else:
        c0_ref, y_ref, pos_ref, p_ref, x1_ref, mod_ref, o_ref = refs
    g = pl.program_id(0)
    e = pl.program_id(1)
    tm = TOKEN_TILE
    d = x1_ref.shape[1]
    gate2 = mod_ref[mod_row:mod_row + 1, 5 * d:6 * d]

    @pl.when(e == 0)
    def _():
        o_ref[...] = x1_ref[...]

    slot = lax.broadcasted_iota(jnp.int32, (tm, kwin), 1)
    col = lax.broadcasted_iota(jnp.int32, (tm, pos_ref.shape[1]), 1) == e
    for u in range(nsub):
        c0 = c0_ref[e * nb + g * nsub + u]
        w0 = pl.multiple_of(jnp.minimum((c0 // 16) * 16, cap - kwin), 16)
        rs = slice(u * tm, (u + 1) * tm)
        slot_of = jnp.sum(jnp.where(col, pos_ref[rs, :].astype(F32), 0.0), axis=1, keepdims=True).astype(jnp.int32)
        gate = jnp.sum(jnp.where(col, p_ref[rs, :], 0.0), axis=1, keepdims=True)
        onehot = jnp.where(slot == slot_of - w0, 1.0, 0.0).astype(BF16)
        r = jnp.dot(onehot, y_ref[pl.ds(w0, kwin), :], preferred_element_type=F32)
        o_ref[rs, :] += gate * (gate2 * r)

    if final:
        @pl.when(e == pl.num_programs(1) - 1)
        def _():
            o_ref[...] = _rms(o_ref[...]) * fg_ref[...]


def _combine_call(c0, y, pos, probs, x1, mod, fg, *, cap, tile0, mod_row):
    ne = y.shape[0]
    t = pos.shape[0]
    tm = TOKEN_TILE
    nb = t // tm
    nsub = min(nb, COMBINE_TILES)
    gt = nsub * tm
    assert nb % nsub == 0 and tile0 % nsub == 0
    d = x1.shape[1]
    kwin = min(2 * tm, cap)
    final = fg is not None
    full = lambda a: pl.BlockSpec(a.shape, lambda g, e, c: (0,) * a.ndim)
    in_specs = [pl.BlockSpec((None, cap, d), lambda g, e, c: (e, 0, 0)),
                pl.BlockSpec((gt, ne), lambda g, e, c: (g, 0)),
                pl.BlockSpec((gt, ne), lambda g, e, c: (g, 0)),
                pl.BlockSpec((gt, d), lambda g, e, c: (tile0 // nsub + g, 0)),
                full(mod)]
    ins = [y, pos, probs, x1, mod]
    if final:
        in_specs.append(full(fg))
        ins.append(fg)
    return pl.pallas_call(
        functools.partial(_combine_kernel, nb=nb, nsub=nsub, kwin=kwin, cap=cap, mod_row=mod_row, final=final),
        grid_spec=pltpu.PrefetchScalarGridSpec(
            num_scalar_prefetch=1,
            grid=(nb // nsub, ne),
            in_specs=in_specs,
            out_specs=pl.BlockSpec((gt, d), lambda g, e, c: (g, 0))),
        out_shape=jax.ShapeDtypeStruct((t, d), F32),
        compiler_params=_cparams(("arbitrary", "arbitrary"), VMEM_LIMIT),
        name="moe_combine",
    )(c0, *ins)


def _rope_tables(t, t_ctx, nh):
    pos = jnp.arange(t)
    r = (pos // GRID_W).astype(F32)
    col = (pos % GRID_W).astype(F32)
    nf = C_ROPE // 4
    inv = ROPE_BASE ** (-jnp.arange(nf, dtype=F32) / nf)
    ar, ac = r[:, None] * inv, col[:, None] * inv
    cos_t = jnp.concatenate([jnp.cos(ar), jnp.cos(ar), jnp.cos(ac), jnp.cos(ac)], axis=-1)
    sin_t = jnp.concatenate([-jnp.sin(ar), jnp.sin(ar), -jnp.sin(ac), jnp.sin(ac)], axis=-1)
    cos_t = jnp.concatenate([cos_t, jnp.ones((t_ctx, C_ROPE), F32)], axis=0)
    sin_t = jnp.concatenate([sin_t, jnp.zeros((t_ctx, C_ROPE), F32)], axis=0)
    return jnp.tile(cos_t, (1, nh)), jnp.tile(sin_t, (1, nh))


def _swap_perm():
    nf = C_ROPE // 4
    idx = jnp.arange(C_ROPE).reshape(2, 2, nf)
    return idx[:, ::-1, :].reshape(-1)


def _moe(h2, probs_t, probs, x1, mod, fg, lp, *, layer, tile0, mod_row):
    ne, t = probs_t.shape
    cap = EC_CAPACITY * t // ne
    pos_t, excl = _select_call(probs_t, cap)
    c0 = excl[:, ::TOKEN_TILE].reshape(-1)
    y = _moe_ffn_call(c0, pos_t, h2, lp['w_gate'], lp['w_up'], lp['w_down'], layer=layer, cap=cap, tile0=tile0)
    return _combine_call(c0, y, pos_t.T, probs, x1, mod, fg, cap=cap, tile0=tile0, mod_row=mod_row)


def kernel(x, c, ctx, c_ctx, norm1_g, w_mod, b_mod, w_in, sgu_w, sgu_b, conv_w, conv_b, lru_wa, lru_ba, lru_wx, lru_bx, lru_lambda, q_norm_g, w_uq, kv_norm_g, w_ukv, w_out, norm2_g, w_router, w_gate, w_up, w_down, final_norm_g):
    bsz, t, d = x.shape
    t_ctx = ctx.shape[1]
    depth = w_mod.shape[0]
    tm = TOKEN_TILE
    assert bsz == 1 and t_ctx == tm and t % Q_TILE == 0 and t % GRID_W == 0
    n = t + t_ctx
    nlt = t // tm
    nt = n // tm
    ql, kvl = q_norm_g.shape[1], kv_norm_g.shape[1]
    nh = w_uq.shape[2] // (C_NOPE + C_ROPE)
    aw = bw = (w_in.shape[2] - ql - kvl - C_ROPE) // 4
    assert (n % KV_CHUNK) == 0

    xl, xc = x[0], ctx[0]
    cv = jnp.zeros((8, d), F32).at[0].set(c[0]).at[1].set(c_ctx)
    mods = _mod_call(cv, w_mod, b_mod)[:, :2]
    cos_t, sin_t = _rope_tables(t, t_ctx, nh)
    perm = _swap_perm()
    row2 = lambda a: a.reshape(1, -1)

    for li in range(depth):
        last = li == depth - 1
        mod = mods[li]
        wi = w_in[li]
        o_kr = 2 * aw + 2 * bw + ql + kvl
        win = jnp.concatenate([wi, wi[:, o_kr:o_kr + C_ROPE][:, perm]], axis=1).astype(BF16)
        wq = w_uq[li].reshape(ql, nh, C_NOPE + C_ROPE)
        wq_r = wq[:, :, C_NOPE:]
        wuq = jnp.concatenate([wq[:, :, :C_NOPE].reshape(ql, -1), wq_r.reshape(ql, -1), wq_r[:, :, perm].reshape(ql, -1)],
                              axis=1).astype(BF16)
        wkv = w_ukv[li].reshape(kvl, nh, C_NOPE + C_V)
        wukv = jnp.concatenate([wkv[:, :, :C_NOPE].reshape(kvl, -1), wkv[:, :, C_NOPE:].reshape(kvl, -1)], axis=1).astype(BF16)
        sguw = sgu_w[li].transpose(1, 0, 2).reshape(CHUNK, A_GROUPS * CHUNK).astype(BF16)
        sgub = jnp.repeat(sgu_b[li].T, aw // A_GROUPS, axis=1)

        ya, xb, gb, q, kt, v = _in_call(xl, xc, mod, row2(norm1_g[li]), win, sguw, sgub, row2(q_norm_g[li]), wuq,
                                       row2(kv_norm_g[li]), wukv, cos_t, sin_t, nh=nh)

        lru = lambda dd: (conv_w[li], row2(conv_b[li]), jax.scipy.linalg.block_diag(*lru_wa[li, dd]).astype(BF16),
                          row2(lru_ba[li, dd]), jax.scipy.linalg.block_diag(*lru_wx[li, dd]).astype(BF16),
                          row2(lru_bx[li, dd]), row2(lru_lambda[li, dd]))
        hf = _lru_call(xb, None, None, *lru(0), reverse=False)
        yb = _lru_call(xb, hf, gb, *lru(1), reverse=True)

        ycl = _attn_call(q, kt, v, q_row0=0, nq_rows=t, kv_row0=0, kv_len=n, tq=Q_TILE, tk=KV_CHUNK)
        ycc = ycl[:t_ctx] if last else _attn_call(q, kt, v, q_row0=t, nq_rows=t_ctx, kv_row0=t, kv_len=t_ctx, tq=t_ctx, tk=t_ctx)

        x1, h2, probs = _out_call(xl, xc, ya, yb, ycl, ycc, w_out[li].astype(BF16), mod, row2(norm2_g[li]), w_router[li],
                                  n_tiles=nlt if last else nt)
        lp = {'w_gate': w_gate, 'w_up': w_up, 'w_down': w_down}
        probs_t = probs.T
        xl = _moe(h2, probs_t[:, :t], probs[:t], x1, mod, row2(final_norm_g) if last else None, lp, layer=li, tile0=0, mod_row=0)
        if not last:
            xc = _moe(h2, probs_t[:, t:], probs[t:], x1, mod, None, lp, layer=li, tile0=nlt, mod_row=1)
    return xl[None]
```

```python
import functools

import jax
import jax.numpy as jnp
import jax.scipy.linalg
from jax import lax
from jax.experimental import pallas as pl
from jax.experimental.pallas import tpu as pltpu

F32 = jnp.float32
BF16 = jnp.bfloat16
HIGHEST = lax.Precision.HIGHEST

GRID_W = 64
CHUNK = 128
EPS = 1e-6
A_GROUPS = 4
CONV_W = 4
LRU_C = 8.0
C_NOPE = 128
C_ROPE = 64
C_V = 128
ROPE_BASE = 10000.0
EC_CAPACITY = 2
LOG2E = 1.4426950408889634
SMALLEST_NORMAL = 1.1754943508222875e-38
BISECT_STEPS = 48

TOKEN_TILE = 256
LANES = 128
Q_TILE = 2048
Q_SUBTILE = 512
KV_CHUNK = 1280
KV_UNROLL = 2
DISPATCH_TILES = 4
DISPATCH_NARROW = 64
COMBINE_TILES = 8
FFN_ROWS = 256
VMEM_LIMIT = 60 * 1024 * 1024


def _cparams(sem, vmem=None):
    return pltpu.CompilerParams(dimension_semantics=sem, vmem_limit_bytes=vmem)


def _rms(x):
    return x * lax.rsqrt(jnp.mean(x * x, axis=-1, keepdims=True) + EPS)


def _mod_kernel(cv_ref, w_ref, b_ref, o_ref):
    cv = cv_ref[...]
    s = cv * jax.nn.sigmoid(cv)
    o_ref[...] = jnp.dot(s, w_ref[...], precision=HIGHEST, preferred_element_type=F32) + b_ref[...]


def _mod_call(cv, w_mod, b_mod):
    depth, d, d6 = w_mod.shape
    tn = 1536
    return pl.pallas_call(
        _mod_kernel,
        grid=(depth, d6 // tn),
        in_specs=[
            pl.BlockSpec((8, d), lambda l, n: (0, 0)),
            pl.BlockSpec((None, d, tn), lambda l, n: (l, 0, n)),
            pl.BlockSpec((None, 1, tn), lambda l, n: (l, 0, n)),
        ],
        out_specs=pl.BlockSpec((None, 8, tn), lambda l, n: (l, 0, n)),
        out_shape=jax.ShapeDtypeStruct((depth, 8, d6), F32),
        compiler_params=_cparams(("arbitrary", "arbitrary")),
        name="modulation",
    )(cv, w_mod, b_mod.reshape(depth, 1, d6))


def _in_kernel(xl_ref, xc_ref, mod_ref, g1_ref, win_ref, sguw_ref, sgub_ref, qg_ref, wuq_ref, kvg_ref, wukv_ref, cos_ref, sin_ref,
               ya_ref, xb_ref, gb_ref, q_ref, kt_ref, v_ref, *, n_lat_tiles, qscale):
    tm, d = xl_ref.shape
    is_ctx = pl.program_id(0) >= n_lat_tiles
    mod = mod_ref[...]
    m = jnp.where(is_ctx, mod[1:2], mod[0:1])
    shift, scale = m[:, 0:d], m[:, d:2 * d]
    h = _rms(jnp.where(is_ctx, xc_ref[...], xl_ref[...])) * g1_ref[...]
    h = h * (1.0 + scale) + shift
    z = jnp.dot(h.astype(BF16), win_ref[...], preferred_element_type=F32)

    aw = sgub_ref.shape[1]
    za = jax.nn.gelu(z[:, 0:2 * aw])
    u, val = za[:, :aw], za[:, aw:]
    vb = _rms(val).astype(BF16)
    gw = aw // A_GROUPS
    lane = lax.broadcasted_iota(jnp.int32, (CHUNK, aw), 1)
    for ci in range(tm // CHUNK):
        vc = vb[ci * CHUNK:(ci + 1) * CHUNK]
        vexp = jnp.concatenate(
            [jnp.where((lane >= g * gw) & (lane < (g + 1) * gw), vc, jnp.zeros_like(vc)) for g in range(A_GROUPS)], axis=0)
        mixed = jnp.dot(sguw_ref[...], vexp, preferred_element_type=F32) + sgub_ref[...]
        ya_ref[ci * CHUNK:(ci + 1) * CHUNK, :] = (u[ci * CHUNK:(ci + 1) * CHUNK] * mixed).astype(ya_ref.dtype)

    o = 2 * aw
    bw = xb_ref.shape[1]
    xb_ref[...] = z[:, o:o + bw]
    gb_ref[...] = jax.nn.gelu(z[:, o + bw:o + 2 * bw])
    o += 2 * bw

    ql = qg_ref.shape[1]
    kvl = kvg_ref.shape[1]
    nh = q_ref.shape[0]
    cos_t, sin_t = cos_ref[...], sin_ref[...]
    cqn = _rms(z[:, o:o + ql]) * qg_ref[...]
    qf = jnp.dot(cqn.astype(BF16), wuq_ref[...], preferred_element_type=F32)
    nn, nr = nh * C_NOPE, nh * C_ROPE
    qr = qf[:, nn:nn + nr] * cos_t + qf[:, nn + nr:nn + 2 * nr] * sin_t
    o += ql
    ckvn = _rms(z[:, o:o + kvl]) * kvg_ref[...]
    kvf = jnp.dot(ckvn.astype(BF16), wukv_ref[...], preferred_element_type=F32)
    o += kvl
    kr = z[:, o:o + C_ROPE] * cos_t[:, :C_ROPE] + z[:, o + C_ROPE:o + 2 * C_ROPE] * sin_t[:, :C_ROPE]
    dk = C_NOPE + C_ROPE
    kr_pad = jnp.concatenate([kr, jnp.zeros((tm, 2 * LANES - dk), F32)], axis=1)
    for hh in range(nh):
        q_ref[hh, :, 0:C_NOPE] = (qf[:, hh * C_NOPE:(hh + 1) * C_NOPE] * qscale).astype(q_ref.dtype)
        q_ref[hh, :, C_NOPE:dk] = (qr[:, hh * C_ROPE:(hh + 1) * C_ROPE] * qscale).astype(q_ref.dtype)
        kh = jnp.concatenate([kvf[:, hh * C_NOPE:(hh + 1) * C_NOPE], kr_pad], axis=1)
        kt_ref[hh] = kh.T[:dk].astype(kt_ref.dtype)
        v_ref[hh] = kvf[:, nn + hh * C_V:nn + (hh + 1) * C_V].astype(v_ref.dtype)


def _in_call(xl, xc, mod, g1, win, sguw, sgub, qg, wuq, kvg, wukv, cos_t, sin_t, *, nh):
    d = xl.shape[1]
    tm = TOKEN_TILE
    n_lat_tiles = xl.shape[0] // tm
    n = xl.shape[0] + xc.shape[0]
    aw = sgub.shape[1]
    bw = aw
    dk = C_NOPE + C_ROPE
    full = lambda a: pl.BlockSpec(a.shape, lambda i: (0,) * a.ndim)
    row = lambda w: pl.BlockSpec((tm, w), lambda i: (i, 0))
    kern = functools.partial(_in_kernel, n_lat_tiles=n_lat_tiles, qscale=float(dk ** -0.5 * LOG2E))
    return pl.pallas_call(
        kern,
        grid=(n // tm,),
        in_specs=[pl.BlockSpec((tm, d), lambda i: (jnp.minimum(i, n_lat_tiles - 1), 0)), full(xc),
                  full(mod), full(g1), full(win), full(sguw), full(sgub), full(qg), full(wuq), full(kvg), full(wukv),
                  row(cos_t.shape[1]), row(sin_t.shape[1])],
        out_specs=[row(aw), row(bw), row(bw),
                   pl.BlockSpec((nh, tm, dk), lambda i: (0, i, 0)),
                   pl.BlockSpec((nh, dk, tm), lambda i: (0, 0, i)),
                   pl.BlockSpec((nh, tm, C_V), lambda i: (0, i, 0))],
        out_shape=[jax.ShapeDtypeStruct((n, aw), BF16), jax.ShapeDtypeStruct((n, bw), F32), jax.ShapeDtypeStruct((n, bw), F32),
                   jax.ShapeDtypeStruct((nh, n, dk), BF16), jax.ShapeDtypeStruct((nh, dk, n), BF16),
                   jax.ShapeDtypeStruct((nh, n, C_V), BF16)],
        compiler_params=_cparams(("arbitrary",), VMEM_LIMIT),
        name="input_proj",
    )(xl, xc, mod, g1, win, sguw, sgub, qg, wuq, kvg, wukv, cos_t, sin_t)


def _tile_scan(a, b, reverse):
    n = a.shape[0]
    rows = lax.broadcasted_iota(jnp.int32, a.shape, 0)
    k = 1
    while k < n:
        sh = n - k if reverse else k
        valid = (rows < n - k) if reverse else (rows >= k)
        a_s = pltpu.roll(a, sh, 0)
        b_s = pltpu.roll(b, sh, 0)
        b = jnp.where(valid, a * b_s, 0.0) + b
        a = jnp.where(valid, a * a_s, a)
        k *= 2
    return a, b


def _softplus(x):
    return jnp.maximum(x, 0.0) + jnp.log(1.0 + jnp.exp(-jnp.abs(x)))


def _lru_kernel(*refs, reverse, nt):
    if reverse:
        xp_ref, x_ref, xn_ref, hf_ref, gb_ref, cw_ref, cb_ref, wa_ref, ba_ref, wx_ref, bx_ref, lam_ref, o_ref, carry_ref = refs
    else:
        xp_ref, x_ref, xn_ref, cw_ref, cb_ref, wa_ref, ba_ref, wx_ref, bx_ref, lam_ref, o_ref, carry_ref = refs
    tm = x_ref.shape[0]
    s = pl.program_id(0)
    tile = _lru_tile(s, nt, reverse)

    @pl.when(s == 0)
    def _():
        carry_ref[...] = jnp.zeros_like(carry_ref)

    prev_ok = jnp.logical_and(tile != 0, tile != nt - 1)
    next_ok = tile < nt - 2
    xp = jnp.where(prev_ok, xp_ref[...], 0.0)
    xn = jnp.where(next_ok, xn_ref[...], 0.0)
    ext = jnp.concatenate([xp, x_ref[...], xn], axis=0)
    cw = cw_ref[...]
    conv = cb_ref[...]
    for kk in range(CONV_W):
        st = 8 + kk - CONV_W // 2
        conv = conv + ext[st:st + tm] * cw[kk:kk + 1]

    cb16 = conv.astype(BF16)
    r = jax.nn.sigmoid(jnp.dot(cb16, wa_ref[...], preferred_element_type=F32) + ba_ref[...])
    gate = jax.nn.sigmoid(jnp.dot(cb16, wx_ref[...], preferred_element_type=F32) + bx_ref[...])
    log_a = (-LRU_C) * r * _softplus(-lam_ref[...])
    a = jnp.exp(log_a)
    mult = jnp.sqrt(jnp.maximum(-jnp.tanh(log_a) * (a * a + 1.0), 0.0))
    b = mult * (gate * conv)
    acum, bcum = _tile_scan(a, b, reverse)
    h = acum * carry_ref[...] + bcum
    carry_ref[...] = h[0:1] if reverse else h[tm - 1:tm]
    if reverse:
        o_ref[...] = ((hf_ref[...] + h) * gb_ref[...]).astype(o_ref.dtype)
    else:
        o_ref[...] = h


def _lru_tile(s, nt, reverse):
    if reverse:
        return jnp.where(s == 0, nt - 1, nt - 1 - s)
    return jnp.where(s == 0, nt - 1, s - 1)


def _lru_call(xb, hf, gb, cw, cb, wa, ba, wx, bx, lam, *, reverse):
    n, w = xb.shape
    tm = TOKEN_TILE
    nt = n // tm
    r8 = tm // 8
    tile = lambda s: _lru_tile(s, nt, reverse)
    row = pl.BlockSpec((tm, w), lambda s: (tile(s), 0))
    prev = pl.BlockSpec((8, w), lambda s: (jnp.maximum(tile(s) * r8 - 1, 0), 0))
    nxt = pl.BlockSpec((8, w), lambda s: (jnp.minimum((tile(s) + 1) * r8, n // 8 - 1), 0))
    full = lambda a: pl.BlockSpec(a.shape, lambda s: (0,) * a.ndim)
    params = [cw, cb, wa, ba, wx, bx, lam]
    ins = [xb, xb, xb] + ([hf, gb] if reverse else []) + params
    specs = [prev, row, nxt] + ([row, row] if reverse else []) + [full(p) for p in params]
    return pl.pallas_call(
        functools.partial(_lru_kernel, reverse=reverse, nt=nt),
        grid=(nt,),
        in_specs=specs,
        out_specs=row,
        out_shape=jax.ShapeDtypeStruct((n, w), BF16 if reverse else F32),
        scratch_shapes=[pltpu.VMEM((1, w), F32)],
        compiler_params=_cparams(("arbitrary",)),
        name="lru_bwd" if reverse else "lru_fwd",
    )(*ins)


def _attn_kernel(*refs, tk, nk, ts):
    q_ref, kt_ref, v_ref, o_ref = refs[:4]
    scratch = refs[4:]
    nsub = q_ref.shape[0] // ts
    m_refs, l_refs, acc_refs = scratch[:nsub], scratch[nsub:2 * nsub], scratch[2 * nsub:]
    lanes = m_refs[0].shape[1]
    nblk = tk // lanes
    for u in range(nsub):
        m_refs[u][...] = jnp.full(m_refs[u].shape, -jnp.inf, F32)
        l_refs[u][...] = jnp.zeros(l_refs[u].shape, F32)
        acc_refs[u][...] = jnp.zeros(acc_refs[u].shape, F32)

    def body(c, carry):
        off = pl.multiple_of(c * tk, tk)
        kt = kt_ref[:, pl.ds(off, tk)]
        v = v_ref[pl.ds(off, tk), :]
        for u in range(nsub):
            q = q_ref[u * ts:(u + 1) * ts, :]
            s = jnp.dot(q, kt, preferred_element_type=F32)
            m_prev = m_refs[u][...]
            m_new = jnp.maximum(m_prev, jnp.max(s, axis=-1, keepdims=True))
            alpha = jnp.exp2(m_prev - m_new)
            pb = [jnp.exp2(s[:, b * lanes:(b + 1) * lanes] - m_new) for b in range(nblk)]
            psum = pb[0]
            for b in range(1, nblk):
                psum = psum + pb[b]
            l_refs[u][...] = alpha * l_refs[u][...] + psum
            p = jnp.concatenate(pb, axis=1).astype(v.dtype)
            acc_refs[u][...] = alpha * acc_refs[u][...] + jnp.dot(p, v, preferred_element_type=F32)
            m_refs[u][...] = m_new
        return carry

    lax.fori_loop(0, nk, body, 0, unroll=min(nk, KV_UNROLL))
    for u in range(nsub):
        l = jnp.sum(l_refs[u][...], axis=-1, keepdims=True)
        o_ref[u * ts:(u + 1) * ts, :] = (acc_refs[u][...] / l).astype(o_ref.dtype)


def _attn_call(q, kt, v, *, q_row0, nq_rows, kv_row0, kv_len, tq, tk):
    nh, _, dk = q.shape
    dv = v.shape[2]
    ts = min(tq, Q_SUBTILE)
    assert q_row0 % tq == 0 and nq_rows % tq == 0 and kv_row0 % kv_len == 0 and kv_len % tk == 0 and tq % ts == 0
    assert dv == LANES and tk % LANES == 0
    nsub = tq // ts
    qb0, kvb = q_row0 // tq, kv_row0 // kv_len
    in_specs = [pl.BlockSpec((None, tq, dk), lambda h, i: (h, qb0 + i, 0)),
                pl.BlockSpec((None, dk, kv_len), lambda h, i: (h, 0, kvb)),
                pl.BlockSpec((None, kv_len, dv), lambda h, i: (h, kvb, 0))]
    return pl.pallas_call(
        functools.partial(_attn_kernel, tk=tk, nk=kv_len // tk, ts=ts),
        grid=(nh, nq_rows // tq),
        in_specs=in_specs,
        out_specs=pl.BlockSpec((tq, dv), lambda h, i: (i, h)),
        out_shape=jax.ShapeDtypeStruct((nq_rows, nh * dv), BF16),
        scratch_shapes=[pltpu.VMEM((ts, LANES), F32)] * (3 * nsub),
        compiler_params=_cparams(("arbitrary", "arbitrary"), VMEM_LIMIT),
        name="attention",
    )(q, kt, v)


def _out_kernel(xl_ref, xc_ref, ya_ref, yb_ref, ycl_ref, ycc_ref, wout_ref, mod_ref, g2_ref, wr_ref, x1_ref, h2_ref, p_ref, *,
                n_lat_tiles):
    d = xl_ref.shape[1]
    is_ctx = pl.program_id(0) >= n_lat_tiles
    x = jnp.where(is_ctx, xc_ref[...], xl_ref[...])
    yc = jnp.where(is_ctx, ycc_ref[...], ycl_ref[...])
    mod = mod_ref[...]
    m = jnp.where(is_ctx, mod[1:2], mod[0:1])
    gate1, shift2, scale2 = m[:, 2 * d:3 * d], m[:, 3 * d:4 * d], m[:, 4 * d:5 * d]
    mix = jnp.concatenate([ya_ref[...], yb_ref[...], yc], axis=-1)
    x1 = x + gate1 * jnp.dot(mix, wout_ref[...], preferred_element_type=F32)
    x1_ref[...] = x1
    h2 = _rms(x1) * g2_ref[...]
    h2 = h2 * (1.0 + scale2) + shift2
    h2_ref[...] = h2.astype(h2_ref.dtype)
    logits = jnp.dot(h2, wr_ref[...], precision=HIGHEST, preferred_element_type=F32)
    ex = jnp.exp(logits - jnp.max(logits, axis=-1, keepdims=True))
    p_ref[...] = ex / jnp.sum(ex, axis=-1, keepdims=True)


def _out_call(xl, xc, ya, yb, ycl, ycc, wout, mod, g2, wr, *, n_tiles):
    d = xl.shape[1]
    tm = TOKEN_TILE
    n_lat_tiles = xl.shape[0] // tm
    n = n_tiles * tm
    lat = lambda w: pl.BlockSpec((tm, w), lambda i: (jnp.minimum(i, n_lat_tiles - 1), 0))
    ne = wr.shape[1]
    full = lambda a: pl.BlockSpec(a.shape, lambda i: (0,) * a.ndim)
    row = lambda w: pl.BlockSpec((tm, w), lambda i: (i, 0))
    return pl.pallas_call(
        functools.partial(_out_kernel, n_lat_tiles=n_lat_tiles),
        grid=(n_tiles,),
        in_specs=[lat(d), full(xc), row(ya.shape[1]), row(yb.shape[1]), lat(ycl.shape[1]), full(ycc),
                  full(wout), full(mod), full(g2), full(wr)],
        out_specs=[row(d), row(d), row(ne)],
        out_shape=[jax.ShapeDtypeStruct((n, d), F32), jax.ShapeDtypeStruct((n, d), BF16), jax.ShapeDtypeStruct((n, ne), F32)],
        compiler_params=_cparams(("arbitrary",), VMEM_LIMIT),
        name="output_proj",
    )(xl, xc, ya, yb, ycl, ycc, wout, mod, g2, wr)


def _cumsum_lanes(x):
    n = x.shape[1]
    lane = lax.broadcasted_iota(jnp.int32, x.shape, 1)
    k = 1
    while k < n:
        x = x + jnp.where(lane >= k, pltpu.roll(x, k, 1), 0)
        k *= 2
    return x


def _count(mask):
    return jnp.sum(jnp.where(mask, 1.0, 0.0), axis=1, keepdims=True).astype(jnp.int32)


def _select_kernel(p_ref, pos_ref, excl_ref, *, cap):
    p = p_ref[...]

    def step(i, bounds):
        lo, hi = bounds
        lo_pos = jnp.maximum(lo, SMALLEST_NORMAL)
        mid = jnp.where(hi > 2.0 * lo_pos, jnp.sqrt(lo_pos) * jnp.sqrt(hi), lo + 0.5 * (hi - lo))
        mid = jnp.clip(mid, lo, hi)
        ok = _count(p >= mid) >= cap
        return jnp.where(ok, mid, lo), jnp.where(ok, hi, mid)

    rows = (p.shape[0], 1)
    lo, hi = lax.fori_loop(0, BISECT_STEPS, step, (jnp.zeros(rows, F32), jnp.full(rows, 2.0, F32)))
    gt = p >= hi
    eq = jnp.logical_and(p >= lo, p < hi)
    need = cap - _count(gt)
    eqi = jnp.where(eq, 1, 0)
    eq_rank = _cumsum_lanes(eqi) - eqi
    sel = jnp.logical_or(gt, jnp.logical_and(eq, eq_rank < need))
    seli = jnp.where(sel, 1, 0)
    excl = _cumsum_lanes(seli) - seli
    excl_ref[...] = excl
    pos_ref[...] = jnp.where(sel, excl, -1)


def _select_call(p_t, cap):
    ne, t = p_t.shape
    return pl.pallas_call(
        functools.partial(_select_kernel, cap=cap),
        out_shape=[jax.ShapeDtypeStruct((ne, t), jnp.int32), jax.ShapeDtypeStruct((ne, t), jnp.int32)],
        compiler_params=pltpu.CompilerParams(vmem_limit_bytes=VMEM_LIMIT),
        name="expert_select",
    )(p_t)


def _moe_ffn_kernel(c0_ref, pos_ref, h_ref, wg_ref, wu_ref, wd_ref, y_ref, xs_ref, *, nb, nsub, win, win_small, cap, rows):
    e = pl.program_id(0)
    j = pl.program_id(1)
    tm = TOKEN_TILE

    @pl.when(j == 0)
    def _():
        xs_ref[...] = jnp.zeros_like(xs_ref)

    base = e * (nb + 1) + j * nsub
    starts = [pl.multiple_of((c0_ref[base + u] // 8) * 8, 8) for u in range(nsub)]

    def dispatch(rows_w):
        slot = lax.broadcasted_iota(jnp.int32, (rows_w, tm), 0)
        for u in range(nsub):
            rel = pos_ref[:, u * tm:(u + 1) * tm] - starts[u]
            onehot = jnp.where(slot == rel, 1.0, 0.0).astype(BF16)
            xs_ref[pl.ds(starts[u], rows_w), :] += jnp.dot(onehot, h_ref[u * tm:(u + 1) * tm, :], preferred_element_type=F32)

    if win_small < win:
        narrow = c0_ref[base + 1] - starts[0] <= win_small
        for u in range(1, nsub):
            narrow = jnp.logical_and(narrow, c0_ref[base + u + 1] - starts[u] <= win_small)
        pl.when(narrow)(lambda: dispatch(win_small))
        pl.when(jnp.logical_not(narrow))(lambda: dispatch(win))
    else:
        dispatch(win)

    @pl.when(j == pl.num_programs(1) - 1)
    def _():
        wg = wg_ref[...].astype(BF16)
        wu = wu_ref[...].astype(BF16)
        wd = wd_ref[...].astype(BF16)
        for ci in range(cap // rows):
            xc = xs_ref[ci * rows:(ci + 1) * rows, :].astype(BF16)
            hg = jnp.dot(xc, wg, preferred_element_type=F32)
            hu = jnp.dot(xc, wu, preferred_element_type=F32)
            hid = (hg * jax.nn.sigmoid(hg) * hu).astype(BF16)
            y_ref[ci * rows:(ci + 1) * rows, :] = jnp.dot(hid, wd, preferred_element_type=F32).astype(y_ref.dtype)


def _moe_ffn_call(c0, pos_t, h2, wg, wu, wd, *, layer, cap, tile0):
    ne, t = pos_t.shape
    tm = TOKEN_TILE
    nb = t // tm
    nsub = min(nb, DISPATCH_TILES)
    gt = nsub * tm
    assert nb % nsub == 0 and tile0 % nsub == 0
    d = h2.shape[1]
    ff = wg.shape[3]
    win = min(tm, cap) + 16
    win_small = min(win, DISPATCH_NARROW)
    rows = min(cap, FFN_ROWS)
    return pl.pallas_call(
        functools.partial(_moe_ffn_kernel, nb=nb, nsub=nsub, win=win, win_small=win_small, cap=cap, rows=rows),
        grid_spec=pltpu.PrefetchScalarGridSpec(
            num_scalar_prefetch=1,
            grid=(ne, nb // nsub),
            in_specs=[pl.BlockSpec((None, 1, gt), lambda e, j, c: (e, 0, j)),
                      pl.BlockSpec((gt, d), lambda e, j, c: (tile0 // nsub + j, 0)),
                      pl.BlockSpec((None, None, d, ff), lambda e, j, c: (layer, e, 0, 0)),
                      pl.BlockSpec((None, None, d, ff), lambda e, j, c: (layer, e, 0, 0)),
                      pl.BlockSpec((None, None, ff, d), lambda e, j, c: (layer, e, 0, 0))],
            out_specs=pl.BlockSpec((None, cap, d), lambda e, j, c: (e, 0, 0)),
            scratch_shapes=[pltpu.VMEM((cap + win, d), F32)]),
        out_shape=jax.ShapeDtypeStruct((ne, cap, d), BF16),
        compiler_params=_cparams(("arbitrary", "arbitrary"), VMEM_LIMIT),
        name="moe_ffn",
    )(c0, pos_t.reshape(ne, 1, t), h2, wg, wu, wd)


def _combine_kernel(*refs, nb, nsub, kwin, kwin_small, cap, mod_row, final):
    if final:
        c0_ref, y_ref, pos_ref, p_ref, x1_ref, mod_ref, fg_ref, o_ref = refs
    else:
        c0_ref, y_ref, pos_ref, p_ref, x1_ref, mod_ref, o_ref = refs
    g = pl.program_id(0)
    e = pl.program_id(1)
    tm = TOKEN_TILE
    d = x1_ref.shape[1]
    gate2 = mod_ref[mod_row:mod_row + 1, 5 * d:6 * d]

    @pl.when(e == 0)
    def _():
        o_ref[...] = x1_ref[...]

    col = lax.broadcasted_iota(jnp.int32, (tm, pos_ref.shape[1]), 1) == e
    base = e * (nb + 1) + g * nsub

    def combine(kw):
        slot = lax.broadcasted_iota(jnp.int32, (tm, kw), 1)
        for u in range(nsub):
            w0 = pl.multiple_of(jnp.minimum((c0_ref[base + u] // 16) * 16, cap - kw), 16)
            rs = slice(u * tm, (u + 1) * tm)
            slot_of = jnp.sum(jnp.where(col, pos_ref[rs, :].astype(F32), 0.0), axis=1, keepdims=True).astype(jnp.int32)
            gate = jnp.sum(jnp.where(col, p_ref[rs, :], 0.0), axis=1, keepdims=True)
            onehot = jnp.where(slot == slot_of - w0, 1.0, 0.0).astype(BF16)
            r = jnp.dot(onehot, y_ref[pl.ds(w0, kw), :], preferred_element_type=F32)
            o_ref[rs, :] += gate * (gate2 * r)

    if kwin_small < kwin:
        narrow = True
        for u in range(nsub):
            w0s = jnp.minimum((c0_ref[base + u] // 16) * 16, cap - kwin_small)
            fits = c0_ref[base + u + 1] - w0s <= kwin_small
            narrow = fits if u == 0 else jnp.logical_and(narrow, fits)
        pl.when(narrow)(lambda: combine(kwin_small))
        pl.when(jnp.logical_not(narrow))(lambda: combine(kwin))
    else:
        combine(kwin)

    if final:
        @pl.when(e == pl.num_programs(1) - 1)
        def _():
            o_ref[...] = _rms(o_ref[...]) * fg_ref[...]


def _combine_call(c0, y, pos, probs, x1, mod, fg, *, cap, tile0, mod_row):
    ne = y.shape[0]
    t = pos.shape[0]
    tm = TOKEN_TILE
    nb = t // tm
    nsub = min(nb, COMBINE_TILES)
    gt = nsub * tm
    assert nb % nsub == 0 and tile0 % nsub == 0
    d = x1.shape[1]
    kwin = min(2 * tm, cap)
    kwin_small = min(tm, kwin)
    final = fg is not None
    full = lambda a: pl.BlockSpec(a.shape, lambda g, e, c: (0,) * a.ndim)
    in_specs = [pl.BlockSpec((None, cap, d), lambda g, e, c: (e, 0, 0)),
                pl.BlockSpec((gt, ne), lambda g, e, c: (g, 0)),
                pl.BlockSpec((gt, ne), lambda g, e, c: (g, 0)),
                pl.BlockSpec((gt, d), lambda g, e, c: (tile0 // nsub + g, 0)),
                full(mod)]
    ins = [y, pos, probs, x1, mod]
    if final:
        in_specs.append(full(fg))
        ins.append(fg)
    return pl.pallas_call(
        functools.partial(_combine_kernel, nb=nb, nsub=nsub, kwin=kwin, kwin_small=kwin_small, cap=cap, mod_row=mod_row,
                          final=final),
        grid_spec=pltpu.PrefetchScalarGridSpec(
            num_scalar_prefetch=1,
            grid=(nb // nsub, ne),
            in_specs=in_specs,
            out_specs=pl.BlockSpec((gt, d), lambda g, e, c: (g, 0))),
        out_shape=jax.ShapeDtypeStruct((t, d), F32),
        compiler_params=_cparams(("arbitrary", "arbitrary"), VMEM_LIMIT),
        name="moe_combine",
    )(c0, *ins)


def _rope_tables(t, t_ctx, nh):
    pos = jnp.arange(t)
    r = (pos // GRID_W).astype(F32)
    col = (pos % GRID_W).astype(F32)
    nf = C_ROPE // 4
    inv = ROPE_BASE ** (-jnp.arange(nf, dtype=F32) / nf)
    ar, ac = r[:, None] * inv, col[:, None] * inv
    cos_t = jnp.concatenate([jnp.cos(ar), jnp.cos(ar), jnp.cos(ac), jnp.cos(ac)], axis=-1)
    sin_t = jnp.concatenate([-jnp.sin(ar), jnp.sin(ar), -jnp.sin(ac), jnp.sin(ac)], axis=-1)
    cos_t = jnp.concatenate([cos_t, jnp.ones((t_ctx, C_ROPE), F32)], axis=0)
    sin_t = jnp.concatenate([sin_t, jnp.zeros((t_ctx, C_ROPE), F32)], axis=0)
    return jnp.tile(cos_t, (1, nh)), jnp.tile(sin_t, (1, nh))


def _swap_perm():
    nf = C_ROPE // 4
    idx = jnp.arange(C_ROPE).reshape(2, 2, nf)
    return idx[:, ::-1, :].reshape(-1)


def _moe(h2, probs_t, probs, x1, mod, fg, lp, *, layer, tile0, mod_row):
    ne, t = probs_t.shape
    cap = EC_CAPACITY * t // ne
    pos_t, excl = _select_call(probs_t, cap)
    c0 = jnp.concatenate([excl[:, ::TOKEN_TILE], jnp.full((ne, 1), cap, jnp.int32)], axis=1).reshape(-1)
    y = _moe_ffn_call(c0, pos_t, h2, lp['w_gate'], lp['w_up'], lp['w_down'], layer=layer, cap=cap, tile0=tile0)
    return _combine_call(c0, y, pos_t.T, probs, x1, mod, fg, cap=cap, tile0=tile0, mod_row=mod_row)


def kernel(x, c, ctx, c_ctx, norm1_g, w_mod, b_mod, w_in, sgu_w, sgu_b, conv_w, conv_b, lru_wa, lru_ba, lru_wx, lru_bx, lru_lambda, q_norm_g, w_uq, kv_norm_g, w_ukv, w_out, norm2_g, w_router, w_gate, w_up, w_down, final_norm_g):
    bsz, t, d = x.shape
    t_ctx = ctx.shape[1]
    depth = w_mod.shape[0]
    tm = TOKEN_TILE
    assert bsz == 1 and t_ctx == tm and t % Q_TILE == 0 and t % GRID_W == 0
    n = t + t_ctx
    nlt = t // tm
    nt = n // tm
    ql, kvl = q_norm_g.shape[1], kv_norm_g.shape[1]
    nh = w_uq.shape[2] // (C_NOPE + C_ROPE)
    aw = bw = (w_in.shape[2] - ql - kvl - C_ROPE) // 4
    assert (n % KV_CHUNK) == 0

    xl, xc = x[0], ctx[0]
    cv = jnp.zeros((8, d), F32).at[0].set(c[0]).at[1].set(c_ctx)
    mods = _mod_call(cv, w_mod, b_mod)[:, :2]
    cos_t, sin_t = _rope_tables(t, t_ctx, nh)
    perm = _swap_perm()
    row2 = lambda a: a.reshape(1, -1)

    for li in range(depth):
        last = li == depth - 1
        mod = mods[li]
        wi = w_in[li]
        o_kr = 2 * aw + 2 * bw + ql + kvl
        win = jnp.concatenate([wi, wi[:, o_kr:o_kr + C_ROPE][:, perm]], axis=1).astype(BF16)
        wq = w_uq[li].reshape(ql, nh, C_NOPE + C_ROPE)
        wq_r = wq[:, :, C_NOPE:]
        wuq = jnp.concatenate([wq[:, :, :C_NOPE].reshape(ql, -1), wq_r.reshape(ql, -1), wq_r[:, :, perm].reshape(ql, -1)],
                              axis=1).astype(BF16)
        wkv = w_ukv[li].reshape(kvl, nh, C_NOPE + C_V)
        wukv = jnp.concatenate([wkv[:, :, :C_NOPE].reshape(kvl, -1), wkv[:, :, C_NOPE:].reshape(kvl, -1)], axis=1).astype(BF16)
        sguw = sgu_w[li].transpose(1, 0, 2).reshape(CHUNK, A_GROUPS * CHUNK).astype(BF16)
        sgub = jnp.repeat(sgu_b[li].T, aw // A_GROUPS, axis=1)

        ya, xb, gb, q, kt, v = _in_call(xl, xc, mod, row2(norm1_g[li]), win, sguw, sgub, row2(q_norm_g[li]), wuq,
                                        row2(kv_norm_g[li]), wukv, cos_t, sin_t, nh=nh)

        lru = lambda dd: (conv_w[li], row2(conv_b[li]), jax.scipy.linalg.block_diag(*lru_wa[li, dd]).astype(BF16),
                          row2(lru_ba[li, dd]), jax.scipy.linalg.block_diag(*lru_wx[li, dd]).astype(BF16),
                          row2(lru_bx[li, dd]), row2(lru_lambda[li, dd]))
        hf = _lru_call(xb, None, None, *lru(0), reverse=False)
        yb = _lru_call(xb, hf, gb, *lru(1), reverse=True)

        ycl = _attn_call(q, kt, v, q_row0=0, nq_rows=t, kv_row0=0, kv_len=n, tq=Q_TILE, tk=KV_CHUNK)
        ycc = ycl[:t_ctx] if last else _attn_call(q, kt, v, q_row0=t, nq_rows=t_ctx, kv_row0=t, kv_len=t_ctx, tq=t_ctx, tk=t_ctx)

        x1, h2, probs = _out_call(xl, xc, ya, yb, ycl, ycc, w_out[li].astype(BF16), mod, row2(norm2_g[li]), w_router[li],
                                  n_tiles=nlt if last else nt)
        lp = {'w_gate': w_gate, 'w_up': w_up, 'w_down': w_down}
        probs_t = probs.T
        xl = _moe(h2, probs_t[:, :t], probs[:t], x1, mod, row2(final_norm_g) if last else None, lp, layer=li, tile0=0, mod_row=0)
        if not last:
            xc = _moe(h2, probs_t[:, t:], probs[t:], x1, mod, None, lp, layer=li, tile0=nlt, mod_row=1)
    return xl[None]
```

```python
import functools

import jax
import jax.numpy as jnp
import jax.scipy.linalg
from jax import lax
from jax.experimental import pallas as pl
from jax.experimental.pallas import tpu as pltpu

F32 = jnp.float32
BF16 = jnp.bfloat16
HIGHEST = lax.Precision.HIGHEST

GRID_W = 64
CHUNK = 128
EPS = 1e-6
A_GROUPS = 4
CONV_W = 4
LRU_C = 8.0
C_NOPE = 128
C_ROPE = 64
C_V = 128
ROPE_BASE = 10000.0
EC_CAPACITY = 2
LOG2E = 1.4426950408889634
SMALLEST_NORMAL = 1.1754943508222875e-38
BISECT_STEPS = 48

TOKEN_TILE = 256
LANES = 128
Q_TILE = 2048
Q_SUBTILE = 512
KV_CHUNK = 1280
KV_UNROLL = 2
DISPATCH_TILES = 4
DISPATCH_NARROW = 64
COMBINE_TILES = 8
FFN_ROWS = 256
VMEM_LIMIT = 60 * 1024 * 1024


def _cparams(sem, vmem=None):
    return pltpu.CompilerParams(dimension_semantics=sem, vmem_limit_bytes=vmem)


def _rms(x):
    return x * lax.rsqrt(jnp.mean(x * x, axis=-1, keepdims=True) + EPS)


def _mod_kernel(cv_ref, w_ref, b_ref, o_ref):
    cv = cv_ref[...]
    s = cv * jax.nn.sigmoid(cv)
    o_ref[...] = jnp.dot(s, w_ref[...], precision=HIGHEST, preferred_element_type=F32) + b_ref[...]


def _mod_call(cv, w_mod, b_mod):
    depth, d, d6 = w_mod.shape
    tn = 1536
    return pl.pallas_call(
        _mod_kernel,
        grid=(depth, d6 // tn),
        in_specs=[
            pl.BlockSpec((8, d), lambda l, n: (0, 0)),
            pl.BlockSpec((None, d, tn), lambda l, n: (l, 0, n)),
            pl.BlockSpec((None, 1, tn), lambda l, n: (l, 0, n)),
        ],
        out_specs=pl.BlockSpec((None, 8, tn), lambda l, n: (l, 0, n)),
        out_shape=jax.ShapeDtypeStruct((depth, 8, d6), F32),
        compiler_params=_cparams(("arbitrary", "arbitrary")),
        name="modulation",
    )(cv, w_mod, b_mod.reshape(depth, 1, d6))


def _in_kernel(xl_ref, xc_ref, mod_ref, g1_ref, win_ref, sguw_ref, sgub_ref, qg_ref, wuq_ref, kvg_ref, wukv_ref, cos_ref, sin_ref,
               ya_ref, xb_ref, gb_ref, q_ref, kt_ref, v_ref, *, n_lat_tiles, qscale):
    tm, d = xl_ref.shape
    is_ctx = pl.program_id(0) >= n_lat_tiles
    mod = mod_ref[...]
    m = jnp.where(is_ctx, mod[1:2], mod[0:1])
    shift, scale = m[:, 0:d], m[:, d:2 * d]
    h = _rms(jnp.where(is_ctx, xc_ref[...], xl_ref[...])) * g1_ref[...]
    h = h * (1.0 + scale) + shift
    z = jnp.dot(h.astype(BF16), win_ref[...], preferred_element_type=F32)

    aw = sgub_ref.shape[1]
    za = jax.nn.gelu(z[:, 0:2 * aw])
    u, val = za[:, :aw], za[:, aw:]
    vb = _rms(val).astype(BF16)
    gw = aw // A_GROUPS
    lane = lax.broadcasted_iota(jnp.int32, (CHUNK, aw), 1)
    for ci in range(tm // CHUNK):
        vc = vb[ci * CHUNK:(ci + 1) * CHUNK]
        vexp = jnp.concatenate(
            [jnp.where((lane >= g * gw) & (lane < (g + 1) * gw), vc, jnp.zeros_like(vc)) for g in range(A_GROUPS)], axis=0)
        mixed = jnp.dot(sguw_ref[...], vexp, preferred_element_type=F32) + sgub_ref[...]
        ya_ref[ci * CHUNK:(ci + 1) * CHUNK, :] = (u[ci * CHUNK:(ci + 1) * CHUNK] * mixed).astype(ya_ref.dtype)

    o = 2 * aw
    bw = xb_ref.shape[1]
    xb_ref[...] = z[:, o:o + bw]
    gb_ref[...] = jax.nn.gelu(z[:, o + bw:o + 2 * bw])
    o += 2 * bw

    ql = qg_ref.shape[1]
    kvl = kvg_ref.shape[1]
    nh = q_ref.shape[0]
    cos_t, sin_t = cos_ref[...], sin_ref[...]
    cqn = _rms(z[:, o:o + ql]) * qg_ref[...]
    qf = jnp.dot(cqn.astype(BF16), wuq_ref[...], preferred_element_type=F32)
    nn, nr = nh * C_NOPE, nh * C_ROPE
    qr = qf[:, nn:nn + nr] * cos_t + qf[:, nn + nr:nn + 2 * nr] * sin_t
    o += ql
    ckvn = _rms(z[:, o:o + kvl]) * kvg_ref[...]
    kvf = jnp.dot(ckvn.astype(BF16), wukv_ref[...], preferred_element_type=F32)
    o += kvl
    kr = z[:, o:o + C_ROPE] * cos_t[:, :C_ROPE] + z[:, o + C_ROPE:o + 2 * C_ROPE] * sin_t[:, :C_ROPE]
    dk = C_NOPE + C_ROPE
    kr_pad = jnp.concatenate([kr, jnp.zeros((tm, 2 * LANES - dk), F32)], axis=1)
    for hh in range(nh):
        q_ref[hh, :, 0:C_NOPE] = (qf[:, hh * C_NOPE:(hh + 1) * C_NOPE] * qscale).astype(q_ref.dtype)
        q_ref[hh, :, C_NOPE:dk] = (qr[:, hh * C_ROPE:(hh + 1) * C_ROPE] * qscale).astype(q_ref.dtype)
        kh = jnp.concatenate([kvf[:, hh * C_NOPE:(hh + 1) * C_NOPE], kr_pad], axis=1)
        kt_ref[hh] = kh.T[:dk].astype(kt_ref.dtype)
        v_ref[hh] = kvf[:, nn + hh * C_V:nn + (hh + 1) * C_V].astype(v_ref.dtype)


def _in_call(xl, xc, mod, g1, win, sguw, sgub, qg, wuq, kvg, wukv, cos_t, sin_t, *, nh):
    d = xl.shape[1]
    tm = TOKEN_TILE
    n_lat_tiles = xl.shape[0] // tm
    n = xl.shape[0] + xc.shape[0]
    aw = sgub.shape[1]
    bw = aw
    dk = C_NOPE + C_ROPE
    full = lambda a: pl.BlockSpec(a.shape, lambda i: (0,) * a.ndim)
    row = lambda w: pl.BlockSpec((tm, w), lambda i: (i, 0))
    kern = functools.partial(_in_kernel, n_lat_tiles=n_lat_tiles, qscale=float(dk ** -0.5 * LOG2E))
    return pl.pallas_call(
        kern,
        grid=(n // tm,),
        in_specs=[pl.BlockSpec((tm, d), lambda i: (jnp.minimum(i, n_lat_tiles - 1), 0)), full(xc),
                  full(mod), full(g1), full(win), full(sguw), full(sgub), full(qg), full(wuq), full(kvg), full(wukv),
                  row(cos_t.shape[1]), row(sin_t.shape[1])],
        out_specs=[row(aw), row(bw), row(bw),
                   pl.BlockSpec((nh, tm, dk), lambda i: (0, i, 0)),
                   pl.BlockSpec((nh, dk, tm), lambda i: (0, 0, i)),
                   pl.BlockSpec((nh, tm, C_V), lambda i: (0, i, 0))],
        out_shape=[jax.ShapeDtypeStruct((n, aw), BF16), jax.ShapeDtypeStruct((n, bw), F32), jax.ShapeDtypeStruct((n, bw), F32),
                   jax.ShapeDtypeStruct((nh, n, dk), BF16), jax.ShapeDtypeStruct((nh, dk, n), BF16),
                   jax.ShapeDtypeStruct((nh, n, C_V), BF16)],
        compiler_params=_cparams(("arbitrary",), VMEM_LIMIT),
        name="input_proj",
    )(xl, xc, mod, g1, win, sguw, sgub, qg, wuq, kvg, wukv, cos_t, sin_t)


def _tile_scan(a, b, reverse):
    n = a.shape[0]
    rows = lax.broadcasted_iota(jnp.int32, a.shape, 0)
    k = 1
    while k < n:
        sh = n - k if reverse else k
        valid = (rows < n - k) if reverse else (rows >= k)
        a_s = pltpu.roll(a, sh, 0)
        b_s = pltpu.roll(b, sh, 0)
        b = jnp.where(valid, a * b_s, 0.0) + b
        a = jnp.where(valid, a * a_s, a)
        k *= 2
    return a, b


def _softplus(x):
    return jnp.maximum(x, 0.0) + jnp.log(1.0 + jnp.exp(-jnp.abs(x)))


def _lru_kernel(*refs, reverse, nt):
    if reverse:
        xp_ref, x_ref, xn_ref, hf_ref, gb_ref, cw_ref, cb_ref, wa_ref, ba_ref, wx_ref, bx_ref, lam_ref, o_ref, carry_ref = refs
    else:
        xp_ref, x_ref, xn_ref, cw_ref, cb_ref, wa_ref, ba_ref, wx_ref, bx_ref, lam_ref, o_ref, carry_ref = refs
    tm = x_ref.shape[0]
    s = pl.program_id(0)
    tile = _lru_tile(s, nt, reverse)

    @pl.when(s == 0)
    def _():
        carry_ref[...] = jnp.zeros_like(carry_ref)

    prev_ok = jnp.logical_and(tile != 0, tile != nt - 1)
    next_ok = tile < nt - 2
    xp = jnp.where(prev_ok, xp_ref[...], 0.0)
    xn = jnp.where(next_ok, xn_ref[...], 0.0)
    ext = jnp.concatenate([xp, x_ref[...], xn], axis=0)
    cw = cw_ref[...]
    conv = cb_ref[...]
    for kk in range(CONV_W):
        st = 8 + kk - CONV_W // 2
        conv = conv + ext[st:st + tm] * cw[kk:kk + 1]

    cb16 = conv.astype(BF16)
    r = jax.nn.sigmoid(jnp.dot(cb16, wa_ref[...], preferred_element_type=F32) + ba_ref[...])
    gate = jax.nn.sigmoid(jnp.dot(cb16, wx_ref[...], preferred_element_type=F32) + bx_ref[...])
    log_a = (-LRU_C) * r * _softplus(-lam_ref[...])
    a = jnp.exp(log_a)
    mult = jnp.sqrt(jnp.maximum(-jnp.tanh(log_a) * (a * a + 1.0), 0.0))
    b = mult * (gate * conv)
    acum, bcum = _tile_scan(a, b, reverse)
    h = acum * carry_ref[...] + bcum
    carry_ref[...] = h[0:1] if reverse else h[tm - 1:tm]
    if reverse:
        o_ref[...] = ((hf_ref[...] + h) * gb_ref[...]).astype(o_ref.dtype)
    else:
        o_ref[...] = h


def _lru_tile(s, nt, reverse):
    if reverse:
        return jnp.where(s == 0, nt - 1, nt - 1 - s)
    return jnp.where(s == 0, nt - 1, s - 1)


def _lru_call(xb, hf, gb, cw, cb, wa, ba, wx, bx, lam, *, reverse):
    n, w = xb.shape
    tm = TOKEN_TILE
    nt = n // tm
    r8 = tm // 8
    tile = lambda s: _lru_tile(s, nt, reverse)
    row = pl.BlockSpec((tm, w), lambda s: (tile(s), 0))
    prev = pl.BlockSpec((8, w), lambda s: (jnp.maximum(tile(s) * r8 - 1, 0), 0))
    nxt = pl.BlockSpec((8, w), lambda s: (jnp.minimum((tile(s) + 1) * r8, n // 8 - 1), 0))
    full = lambda a: pl.BlockSpec(a.shape, lambda s: (0,) * a.ndim)
    params = [cw, cb, wa, ba, wx, bx, lam]
    ins = [xb, xb, xb] + ([hf, gb] if reverse else []) + params
    specs = [prev, row, nxt] + ([row, row] if reverse else []) + [full(p) for p in params]
    return pl.pallas_call(
        functools.partial(_lru_kernel, reverse=reverse, nt=nt),
        grid=(nt,),
        in_specs=specs,
        out_specs=row,
        out_shape=jax.ShapeDtypeStruct((n, w), BF16 if reverse else F32),
        scratch_shapes=[pltpu.VMEM((1, w), F32)],
        compiler_params=_cparams(("arbitrary",)),
        name="lru_bwd" if reverse else "lru_fwd",
    )(*ins)


def _attn_kernel(*refs, tk, nk, ts):
    q_ref, kt_ref, v_ref, o_ref = refs[:4]
    scratch = refs[4:]
    nsub = q_ref.shape[0] // ts
    m_refs, l_refs, acc_refs = scratch[:nsub], scratch[nsub:2 * nsub], scratch[2 * nsub:]
    lanes = m_refs[0].shape[1]
    nblk = tk // lanes
    for u in range(nsub):
        m_refs[u][...] = jnp.full(m_refs[u].shape, -jnp.inf, F32)
        l_refs[u][...] = jnp.zeros(l_refs[u].shape, F32)
        acc_refs[u][...] = jnp.zeros(acc_refs[u].shape, F32)

    def body(c, carry):
        off = pl.multiple_of(c * tk, tk)
        kt = kt_ref[:, pl.ds(off, tk)]
        v = v_ref[pl.ds(off, tk), :]
        for u in range(nsub):
            q = q_ref[u * ts:(u + 1) * ts, :]
            s = jnp.dot(q, kt, preferred_element_type=F32)
            m_prev = m_refs[u][...]
            m_new = jnp.maximum(m_prev, jnp.max(s, axis=-1, keepdims=True))
            alpha = jnp.exp2(m_prev - m_new)
            pb = [jnp.exp2(s[:, b * lanes:(b + 1) * lanes] - m_new) for b in range(nblk)]
            psum = pb[0]
            for b in range(1, nblk):
                psum = psum + pb[b]
            l_refs[u][...] = alpha * l_refs[u][...] + psum
            p = jnp.concatenate(pb, axis=1).astype(v.dtype)
            acc_refs[u][...] = alpha * acc_refs[u][...] + jnp.dot(p, v, preferred_element_type=F32)
            m_refs[u][...] = m_new
        return carry

    lax.fori_loop(0, nk, body, 0, unroll=min(nk, KV_UNROLL))
    for u in range(nsub):
        l = jnp.sum(l_refs[u][...], axis=-1, keepdims=True)
        o_ref[u * ts:(u + 1) * ts, :] = (acc_refs[u][...] / l).astype(o_ref.dtype)


def _attn_call(q, kt, v, *, q_row0, nq_rows, kv_row0, kv_len, tq, tk):
    nh, _, dk = q.shape
    dv = v.shape[2]
    ts = min(tq, Q_SUBTILE)
    assert q_row0 % tq == 0 and nq_rows % tq == 0 and kv_row0 % kv_len == 0 and kv_len % tk == 0 and tq % ts == 0
    assert dv == LANES and tk % LANES == 0
    nsub = tq // ts
    qb0, kvb = q_row0 // tq, kv_row0 // kv_len
    in_specs = [pl.BlockSpec((None, tq, dk), lambda h, i: (h, qb0 + i, 0)),
                pl.BlockSpec((None, dk, kv_len), lambda h, i: (h, 0, kvb)),
                pl.BlockSpec((None, kv_len, dv), lambda h, i: (h, kvb, 0))]
    return pl.pallas_call(
        functools.partial(_attn_kernel, tk=tk, nk=kv_len // tk, ts=ts),
        grid=(nh, nq_rows // tq),
        in_specs=in_specs,
        out_specs=pl.BlockSpec((tq, dv), lambda h, i: (i, h)),
        out_shape=jax.ShapeDtypeStruct((nq_rows, nh * dv), BF16),
        scratch_shapes=[pltpu.VMEM((ts, LANES), F32)] * (3 * nsub),
        compiler_params=_cparams(("arbitrary", "arbitrary"), VMEM_LIMIT),
        name="attention",
    )(q, kt, v)


def _out_kernel(xl_ref, xc_ref, ya_ref, yb_ref, ycl_ref, ycc_ref, wout_ref, mod_ref, g2_ref, wr_ref, x1_ref, h2_ref, p_ref, *,
                n_lat_tiles):
    d = xl_ref.shape[1]
    is_ctx = pl.program_id(0) >= n_lat_tiles
    x = jnp.where(is_ctx, xc_ref[...], xl_ref[...])
    yc = jnp.where(is_ctx, ycc_ref[...], ycl_ref[...])
    mod = mod_ref[...]
    m = jnp.where(is_ctx, mod[1:2], mod[0:1])
    gate1, shift2, scale2 = m[:, 2 * d:3 * d], m[:, 3 * d:4 * d], m[:, 4 * d:5 * d]
    mix = jnp.concatenate([ya_ref[...], yb_ref[...], yc], axis=-1)
    x1 = x + gate1 * jnp.dot(mix, wout_ref[...], preferred_element_type=F32)
    x1_ref[...] = x1
    h2 = _rms(x1) * g2_ref[...]
    h2 = h2 * (1.0 + scale2) + shift2
    h2_ref[...] = h2.astype(h2_ref.dtype)
    ne = p_ref.shape[1]
    h_hi = h2.astype(BF16)
    h_lo = (h2 - h_hi.astype(F32)).astype(BF16)
    wr = wr_ref[...]
    part = jnp.dot(h_hi, wr, preferred_element_type=F32)
    logits = part[:, :ne] + part[:, ne:] + jnp.dot(h_lo, wr[:, :ne], preferred_element_type=F32)
    ex = jnp.exp(logits - jnp.max(logits, axis=-1, keepdims=True))
    p_ref[...] = ex / jnp.sum(ex, axis=-1, keepdims=True)


def _out_call(xl, xc, ya, yb, ycl, ycc, wout, mod, g2, wr, *, n_tiles):
    d = xl.shape[1]
    tm = TOKEN_TILE
    n_lat_tiles = xl.shape[0] // tm
    n = n_tiles * tm
    lat = lambda w: pl.BlockSpec((tm, w), lambda i: (jnp.minimum(i, n_lat_tiles - 1), 0))
    ne = wr.shape[1] // 2
    full = lambda a: pl.BlockSpec(a.shape, lambda i: (0,) * a.ndim)
    row = lambda w: pl.BlockSpec((tm, w), lambda i: (i, 0))
    return pl.pallas_call(
        functools.partial(_out_kernel, n_lat_tiles=n_lat_tiles),
        grid=(n_tiles,),
        in_specs=[lat(d), full(xc), row(ya.shape[1]), row(yb.shape[1]), lat(ycl.shape[1]), full(ycc),
                  full(wout), full(mod), full(g2), full(wr)],
        out_specs=[row(d), row(d), row(ne)],
        out_shape=[jax.ShapeDtypeStruct((n, d), F32), jax.ShapeDtypeStruct((n, d), BF16), jax.ShapeDtypeStruct((n, ne), F32)],
        compiler_params=_cparams(("arbitrary",), VMEM_LIMIT),
        name="output_proj",
    )(xl, xc, ya, yb, ycl, ycc, wout, mod, g2, wr)


def _cumsum_lanes(x):
    n = x.shape[1]
    lane = lax.broadcasted_iota(jnp.int32, x.shape, 1)
    k = 1
    while k < n:
        x = x + jnp.where(lane >= k, pltpu.roll(x, k, 1), 0)
        k *= 2
    return x


def _count(mask):
    return jnp.sum(jnp.where(mask, 1.0, 0.0), axis=1, keepdims=True).astype(jnp.int32)


def _select_kernel(p_ref, pos_ref, excl_ref, *, cap):
    p = p_ref[...]

    def step(i, bounds):
        lo, hi = bounds
        lo_pos = jnp.maximum(lo, SMALLEST_NORMAL)
        mid = jnp.where(hi > 2.0 * lo_pos, jnp.sqrt(lo_pos) * jnp.sqrt(hi), lo + 0.5 * (hi - lo))
        mid = jnp.clip(mid, lo, hi)
        ok = _count(p >= mid) >= cap
        return jnp.where(ok, mid, lo), jnp.where(ok, hi, mid)

    rows = (p.shape[0], 1)
    lo, hi = lax.fori_loop(0, BISECT_STEPS, step, (jnp.zeros(rows, F32), jnp.full(rows, 2.0, F32)))
    gt = p >= hi
    eq = jnp.logical_and(p >= lo, p < hi)
    need = cap - _count(gt)
    eqi = jnp.where(eq, 1, 0)
    eq_rank = _cumsum_lanes(eqi) - eqi
    sel = jnp.logical_or(gt, jnp.logical_and(eq, eq_rank < need))
    seli = jnp.where(sel, 1, 0)
    excl = _cumsum_lanes(seli) - seli
    excl_ref[...] = excl
    pos_ref[...] = jnp.where(sel, excl, -1)


def _select_call(p_t, cap):
    ne, t = p_t.shape
    return pl.pallas_call(
        functools.partial(_select_kernel, cap=cap),
        out_shape=[jax.ShapeDtypeStruct((ne, t), jnp.int32), jax.ShapeDtypeStruct((ne, t), jnp.int32)],
        compiler_params=pltpu.CompilerParams(vmem_limit_bytes=VMEM_LIMIT),
        name="expert_select",
    )(p_t)


def _moe_ffn_kernel(c0_ref, pos_ref, h_ref, wg_ref, wu_ref, wd_ref, y_ref, xs_ref, *, nb, nsub, win, win_small, cap, rows):
    e = pl.program_id(0)
    j = pl.program_id(1)
    tm = TOKEN_TILE

    @pl.when(j == 0)
    def _():
        xs_ref[...] = jnp.zeros_like(xs_ref)

    base = e * (nb + 1) + j * nsub
    starts = [pl.multiple_of((c0_ref[base + u] // 8) * 8, 8) for u in range(nsub)]

    def dispatch(rows_w):
        slot = lax.broadcasted_iota(jnp.int32, (rows_w, tm), 0)
        for u in range(nsub):
            rel = pos_ref[:, u * tm:(u + 1) * tm] - starts[u]
            onehot = jnp.where(slot == rel, 1.0, 0.0).astype(BF16)
            xs_ref[pl.ds(starts[u], rows_w), :] += jnp.dot(onehot, h_ref[u * tm:(u + 1) * tm, :], preferred_element_type=F32)

    if win_small < win:
        narrow = c0_ref[base + 1] - starts[0] <= win_small
        for u in range(1, nsub):
            narrow = jnp.logical_and(narrow, c0_ref[base + u + 1] - starts[u] <= win_small)
        pl.when(narrow)(lambda: dispatch(win_small))
        pl.when(jnp.logical_not(narrow))(lambda: dispatch(win))
    else:
        dispatch(win)

    @pl.when(j == pl.num_programs(1) - 1)
    def _():
        wg = wg_ref[...].astype(BF16)
        wu = wu_ref[...].astype(BF16)
        wd = wd_ref[...].astype(BF16)
        for ci in range(cap // rows):
            xc = xs_ref[ci * rows:(ci + 1) * rows, :].astype(BF16)
            hg = jnp.dot(xc, wg, preferred_element_type=F32)
            hu = jnp.dot(xc, wu, preferred_element_type=F32)
            hid = (hg * jax.nn.sigmoid(hg) * hu).astype(BF16)
            y_ref[ci * rows:(ci + 1) * rows, :] = jnp.dot(hid, wd, preferred_element_type=F32).astype(y_ref.dtype)


def _moe_ffn_call(c0, pos_t, h2, wg, wu, wd, *, layer, cap, tile0):
    ne, t = pos_t.shape
    tm = TOKEN_TILE
    nb = t // tm
    nsub = min(nb, DISPATCH_TILES)
    gt = nsub * tm
    assert nb % nsub == 0 and tile0 % nsub == 0
    d = h2.shape[1]
    ff = wg.shape[3]
    win = min(tm, cap) + 16
    win_small = min(win, DISPATCH_NARROW)
    rows = min(cap, FFN_ROWS)
    return pl.pallas_call(
        functools.partial(_moe_ffn_kernel, nb=nb, nsub=nsub, win=win, win_small=win_small, cap=cap, rows=rows),
        grid_spec=pltpu.PrefetchScalarGridSpec(
            num_scalar_prefetch=1,
            grid=(ne, nb // nsub),
            in_specs=[pl.BlockSpec((None, 1, gt), lambda e, j, c: (e, 0, j)),
                      pl.BlockSpec((gt, d), lambda e, j, c: (tile0 // nsub + j, 0)),
                      pl.BlockSpec((None, None, d, ff), lambda e, j, c: (layer, e, 0, 0)),
                      pl.BlockSpec((None, None, d, ff), lambda e, j, c: (layer, e, 0, 0)),
                      pl.BlockSpec((None, None, ff, d), lambda e, j, c: (layer, e, 0, 0))],
            out_specs=pl.BlockSpec((None, cap, d), lambda e, j, c: (e, 0, 0)),
            scratch_shapes=[pltpu.VMEM((cap + win, d), F32)]),
        out_shape=jax.ShapeDtypeStruct((ne, cap, d), BF16),
        compiler_params=_cparams(("arbitrary", "arbitrary"), VMEM_LIMIT),
        name="moe_ffn",
    )(c0, pos_t.reshape(ne, 1, t), h2, wg, wu, wd)


def _combine_kernel(*refs, nb, nsub, kwin, kwin_small, cap, mod_row, final):
    if final:
        c0_ref, y_ref, pos_ref, p_ref, x1_ref, mod_ref, fg_ref, o_ref = refs
    else:
        c0_ref, y_ref, pos_ref, p_ref, x1_ref, mod_ref, o_ref = refs
    g = pl.program_id(0)
    e = pl.program_id(1)
    tm = TOKEN_TILE
    d = x1_ref.shape[1]
    gate2 = mod_ref[mod_row:mod_row + 1, 5 * d:6 * d]

    @pl.when(e == 0)
    def _():
        o_ref[...] = x1_ref[...]

    col = lax.broadcasted_iota(jnp.int32, (tm, pos_ref.shape[1]), 1) == e
    base = e * (nb + 1) + g * nsub

    def combine(kw):
        slot = lax.broadcasted_iota(jnp.int32, (tm, kw), 1)
        for u in range(nsub):
            w0 = pl.multiple_of(jnp.minimum((c0_ref[base + u] // 16) * 16, cap - kw), 16)
            rs = slice(u * tm, (u + 1) * tm)
            slot_of = jnp.sum(jnp.where(col, pos_ref[rs, :].astype(F32), 0.0), axis=1, keepdims=True).astype(jnp.int32)
            gate = jnp.sum(jnp.where(col, p_ref[rs, :], 0.0), axis=1, keepdims=True)
            onehot = jnp.where(slot == slot_of - w0, 1.0, 0.0).astype(BF16)
            r = jnp.dot(onehot, y_ref[pl.ds(w0, kw), :], preferred_element_type=F32)
            o_ref[rs, :] += gate * (gate2 * r)

    if kwin_small < kwin:
        narrow = True
        for u in range(nsub):
            w0s = jnp.minimum((c0_ref[base + u] // 16) * 16, cap - kwin_small)
            fits = c0_ref[base + u + 1] - w0s <= kwin_small
            narrow = fits if u == 0 else jnp.logical_and(narrow, fits)
        pl.when(narrow)(lambda: combine(kwin_small))
        pl.when(jnp.logical_not(narrow))(lambda: combine(kwin))
    else:
        combine(kwin)

    if final:
        @pl.when(e == pl.num_programs(1) - 1)
        def _():
            o_ref[...] = _rms(o_ref[...]) * fg_ref[...]


def _combine_call(c0, y, pos, probs, x1, mod, fg, *, cap, tile0, mod_row):
    ne = y.shape[0]
    t = pos.shape[0]
    tm = TOKEN_TILE
    nb = t // tm
    nsub = min(nb, COMBINE_TILES)
    gt = nsub * tm
    assert nb % nsub == 0 and tile0 % nsub == 0
    d = x1.shape[1]
    kwin = min(2 * tm, cap)
    kwin_small = min(tm, kwin)
    final = fg is not None
    full = lambda a: pl.BlockSpec(a.shape, lambda g, e, c: (0,) * a.ndim)
    in_specs = [pl.BlockSpec((None, cap, d), lambda g, e, c: (e, 0, 0)),
                pl.BlockSpec((gt, ne), lambda g, e, c: (g, 0)),
                pl.BlockSpec((gt, ne), lambda g, e, c: (g, 0)),
                pl.BlockSpec((gt, d), lambda g, e, c: (tile0 // nsub + g, 0)),
                full(mod)]
    ins = [y, pos, probs, x1, mod]
    if final:
        in_specs.append(full(fg))
        ins.append(fg)
    return pl.pallas_call(
        functools.partial(_combine_kernel, nb=nb, nsub=nsub, kwin=kwin, kwin_small=kwin_small, cap=cap, mod_row=mod_row,
                          final=final),
        grid_spec=pltpu.PrefetchScalarGridSpec(
            num_scalar_prefetch=1,
            grid=(nb // nsub, ne),
            in_specs=in_specs,
            out_specs=pl.BlockSpec((gt, d), lambda g, e, c: (g, 0))),
        out_shape=jax.ShapeDtypeStruct((t, d), F32),
        compiler_params=_cparams(("arbitrary", "arbitrary"), VMEM_LIMIT),
        name="moe_combine",
    )(c0, *ins)


def _rope_tables(t, t_ctx, nh):
    pos = jnp.arange(t)
    r = (pos // GRID_W).astype(F32)
    col = (pos % GRID_W).astype(F32)
    nf = C_ROPE // 4
    inv = ROPE_BASE ** (-jnp.arange(nf, dtype=F32) / nf)
    ar, ac = r[:, None] * inv, col[:, None] * inv
    cos_t = jnp.concatenate([jnp.cos(ar), jnp.cos(ar), jnp.cos(ac), jnp.cos(ac)], axis=-1)
    sin_t = jnp.concatenate([-jnp.sin(ar), jnp.sin(ar), -jnp.sin(ac), jnp.sin(ac)], axis=-1)
    cos_t = jnp.concatenate([cos_t, jnp.ones((t_ctx, C_ROPE), F32)], axis=0)
    sin_t = jnp.concatenate([sin_t, jnp.zeros((t_ctx, C_ROPE), F32)], axis=0)
    return jnp.tile(cos_t, (1, nh)), jnp.tile(sin_t, (1, nh))


def _swap_perm():
    nf = C_ROPE // 4
    idx = jnp.arange(C_ROPE).reshape(2, 2, nf)
    return idx[:, ::-1, :].reshape(-1)


def _moe(h2, probs_t, probs, x1, mod, fg, lp, *, layer, tile0, mod_row):
    ne, t = probs_t.shape
    cap = EC_CAPACITY * t // ne
    pos_t, excl = _select_call(probs_t, cap)
    c0 = jnp.concatenate([excl[:, ::TOKEN_TILE], jnp.full((ne, 1), cap, jnp.int32)], axis=1).reshape(-1)
    y = _moe_ffn_call(c0, pos_t, h2, lp['w_gate'], lp['w_up'], lp['w_down'], layer=layer, cap=cap, tile0=tile0)
    return _combine_call(c0, y, pos_t.T, probs, x1, mod, fg, cap=cap, tile0=tile0, mod_row=mod_row)


def kernel(x, c, ctx, c_ctx, norm1_g, w_mod, b_mod, w_in, sgu_w, sgu_b, conv_w, conv_b, lru_wa, lru_ba, lru_wx, lru_bx, lru_lambda, q_norm_g, w_uq, kv_norm_g, w_ukv, w_out, norm2_g, w_router, w_gate, w_up, w_down, final_norm_g):
    bsz, t, d = x.shape
    t_ctx = ctx.shape[1]
    depth = w_mod.shape[0]
    tm = TOKEN_TILE
    assert bsz == 1 and t_ctx == tm and t % Q_TILE == 0 and t % GRID_W == 0
    n = t + t_ctx
    nlt = t // tm
    nt = n // tm
    ql, kvl = q_norm_g.shape[1], kv_norm_g.shape[1]
    nh = w_uq.shape[2] // (C_NOPE + C_ROPE)
    aw = bw = (w_in.shape[2] - ql - kvl - C_ROPE) // 4
    assert (n % KV_CHUNK) == 0

    xl, xc = x[0], ctx[0]
    cv = jnp.zeros((8, d), F32).at[0].set(c[0]).at[1].set(c_ctx)
    mods = _mod_call(cv, w_mod, b_mod)[:, :2]
    cos_t, sin_t = _rope_tables(t, t_ctx, nh)
    perm = _swap_perm()
    row2 = lambda a: a.reshape(1, -1)

    for li in range(depth):
        last = li == depth - 1
        mod = mods[li]
        wi = w_in[li]
        o_kr = 2 * aw + 2 * bw + ql + kvl
        win = jnp.concatenate([wi, wi[:, o_kr:o_kr + C_ROPE][:, perm]], axis=1).astype(BF16)
        wq = w_uq[li].reshape(ql, nh, C_NOPE + C_ROPE)
        wq_r = wq[:, :, C_NOPE:]
        wuq = jnp.concatenate([wq[:, :, :C_NOPE].reshape(ql, -1), wq_r.reshape(ql, -1), wq_r[:, :, perm].reshape(ql, -1)],
                              axis=1).astype(BF16)
        wkv = w_ukv[li].reshape(kvl, nh, C_NOPE + C_V)
        wukv = jnp.concatenate([wkv[:, :, :C_NOPE].reshape(kvl, -1), wkv[:, :, C_NOPE:].reshape(kvl, -1)], axis=1).astype(BF16)
        sguw = sgu_w[li].transpose(1, 0, 2).reshape(CHUNK, A_GROUPS * CHUNK).astype(BF16)
        sgub = jnp.repeat(sgu_b[li].T, aw // A_GROUPS, axis=1)

        ya, xb, gb, q, kt, v = _in_call(xl, xc, mod, row2(norm1_g[li]), win, sguw, sgub, row2(q_norm_g[li]), wuq,
                                        row2(kv_norm_g[li]), wukv, cos_t, sin_t, nh=nh)

        lru = lambda dd: (conv_w[li], row2(conv_b[li]), jax.scipy.linalg.block_diag(*lru_wa[li, dd]).astype(BF16),
                          row2(lru_ba[li, dd]), jax.scipy.linalg.block_diag(*lru_wx[li, dd]).astype(BF16),
                          row2(lru_bx[li, dd]), row2(lru_lambda[li, dd]))
        hf = _lru_call(xb, None, None, *lru(0), reverse=False)
        yb = _lru_call(xb, hf, gb, *lru(1), reverse=True)

        ycl = _attn_call(q, kt, v, q_row0=0, nq_rows=t, kv_row0=0, kv_len=n, tq=Q_TILE, tk=KV_CHUNK)
        ycc = ycl[:t_ctx] if last else _attn_call(q, kt, v, q_row0=t, nq_rows=t_ctx, kv_row0=t, kv_len=t_ctx, tq=t_ctx, tk=t_ctx)

        wr_hi = w_router[li].astype(BF16)
        wr_split = jnp.concatenate([wr_hi, (w_router[li] - wr_hi.astype(F32)).astype(BF16)], axis=1)
        x1, h2, probs = _out_call(xl, xc, ya, yb, ycl, ycc, w_out[li].astype(BF16), mod, row2(norm2_g[li]), wr_split,
                                  n_tiles=nlt if last else nt)
        lp = {'w_gate': w_gate, 'w_up': w_up, 'w_down': w_down}
        probs_t = probs.T
        xl = _moe(h2, probs_t[:, :t], probs[:t], x1, mod, row2(final_norm_g) if last else None, lp, layer=li, tile0=0, mod_row=0)
        if not last:
            xc = _moe(h2, probs_t[:, t:], probs[t:], x1, mod, None, lp, layer=li, tile0=nlt, mod_row=1)
    return xl[None]
```

```python
import functools

import jax
import jax.numpy as jnp
import jax.scipy.linalg
from jax import lax
from jax.experimental import pallas as pl
from jax.experimental.pallas import tpu as pltpu

F32 = jnp.float32
BF16 = jnp.bfloat16
HIGHEST = lax.Precision.HIGHEST

GRID_W = 64
CHUNK = 128
EPS = 1e-6
A_GROUPS = 4
CONV_W = 4
LRU_C = 8.0
C_NOPE = 128
C_ROPE = 64
C_V = 128
ROPE_BASE = 10000.0
EC_CAPACITY = 2
LOG2E = 1.4426950408889634
SMALLEST_NORMAL = 1.1754943508222875e-38
BISECT_STEPS = 48

TOKEN_TILE = 256
LANES = 128
Q_TILE = 2048
Q_SUBTILE = 512
KV_CHUNK = 1280
KV_UNROLL = 2
DISPATCH_TILES = 8
DISPATCH_NARROW = 64
COMBINE_TILES = 8
FFN_ROWS = 256
VMEM_LIMIT = 60 * 1024 * 1024


def _cparams(sem, vmem=None):
    return pltpu.CompilerParams(dimension_semantics=sem, vmem_limit_bytes=vmem)


def _rms(x):
    return x * lax.rsqrt(jnp.mean(x * x, axis=-1, keepdims=True) + EPS)


def _mod_kernel(cv_ref, w_ref, b_ref, o_ref):
    cv = cv_ref[...]
    s = cv * jax.nn.sigmoid(cv)
    o_ref[...] = jnp.dot(s, w_ref[...], precision=HIGHEST, preferred_element_type=F32) + b_ref[...]


def _mod_call(cv, w_mod, b_mod):
    depth, d, d6 = w_mod.shape
    tn = 1536
    return pl.pallas_call(
        _mod_kernel,
        grid=(depth, d6 // tn),
        in_specs=[
            pl.BlockSpec((8, d), lambda l, n: (0, 0)),
            pl.BlockSpec((None, d, tn), lambda l, n: (l, 0, n)),
            pl.BlockSpec((None, 1, tn), lambda l, n: (l, 0, n)),
        ],
        out_specs=pl.BlockSpec((None, 8, tn), lambda l, n: (l, 0, n)),
        out_shape=jax.ShapeDtypeStruct((depth, 8, d6), F32),
        compiler_params=_cparams(("arbitrary", "arbitrary")),
        name="modulation",
    )(cv, w_mod, b_mod.reshape(depth, 1, d6))


def _in_kernel(xl_ref, xc_ref, mod_ref, g1_ref, win_ref, sguw_ref, sgub_ref, qg_ref, wuq_ref, kvg_ref, wukv_ref, cos_ref, sin_ref,
               ya_ref, xb_ref, gb_ref, q_ref, kt_ref, v_ref, *, n_lat_tiles, qscale):
    tm, d = xl_ref.shape
    is_ctx = pl.program_id(0) >= n_lat_tiles
    mod = mod_ref[...]
    m = jnp.where(is_ctx, mod[1:2], mod[0:1])
    shift, scale = m[:, 0:d], m[:, d:2 * d]
    h = _rms(jnp.where(is_ctx, xc_ref[...], xl_ref[...])) * g1_ref[...]
    h = h * (1.0 + scale) + shift
    z = jnp.dot(h.astype(BF16), win_ref[...], preferred_element_type=F32)

    aw = sgub_ref.shape[1]
    za = jax.nn.gelu(z[:, 0:2 * aw])
    u, val = za[:, :aw], za[:, aw:]
    vb = _rms(val).astype(BF16)
    gw = aw // A_GROUPS
    lane = lax.broadcasted_iota(jnp.int32, (CHUNK, aw), 1)
    for ci in range(tm // CHUNK):
        vc = vb[ci * CHUNK:(ci + 1) * CHUNK]
        vexp = jnp.concatenate(
            [jnp.where((lane >= g * gw) & (lane < (g + 1) * gw), vc, jnp.zeros_like(vc)) for g in range(A_GROUPS)], axis=0)
        mixed = jnp.dot(sguw_ref[...], vexp, preferred_element_type=F32) + sgub_ref[...]
        ya_ref[ci * CHUNK:(ci + 1) * CHUNK, :] = (u[ci * CHUNK:(ci + 1) * CHUNK] * mixed).astype(ya_ref.dtype)

    o = 2 * aw
    bw = xb_ref.shape[1]
    xb_ref[...] = z[:, o:o + bw]
    gb_ref[...] = jax.nn.gelu(z[:, o + bw:o + 2 * bw])
    o += 2 * bw

    ql = qg_ref.shape[1]
    kvl = kvg_ref.shape[1]
    nh = q_ref.shape[0]
    cos_t, sin_t = cos_ref[...], sin_ref[...]
    cqn = _rms(z[:, o:o + ql]) * qg_ref[...]
    qf = jnp.dot(cqn.astype(BF16), wuq_ref[...], preferred_element_type=F32)
    nn, nr = nh * C_NOPE, nh * C_ROPE
    qr = qf[:, nn:nn + nr] * cos_t + qf[:, nn + nr:nn + 2 * nr] * sin_t
    o += ql
    ckvn = _rms(z[:, o:o + kvl]) * kvg_ref[...]
    kvf = jnp.dot(ckvn.astype(BF16), wukv_ref[...], preferred_element_type=F32)
    o += kvl
    kr = z[:, o:o + C_ROPE] * cos_t[:, :C_ROPE] + z[:, o + C_ROPE:o + 2 * C_ROPE] * sin_t[:, :C_ROPE]
    dk = C_NOPE + C_ROPE
    kr_pad = jnp.concatenate([kr, jnp.zeros((tm, 2 * LANES - dk), F32)], axis=1)
    for hh in range(nh):
        q_ref[hh, :, 0:C_NOPE] = (qf[:, hh * C_NOPE:(hh + 1) * C_NOPE] * qscale).astype(q_ref.dtype)
        q_ref[hh, :, C_NOPE:dk] = (qr[:, hh * C_ROPE:(hh + 1) * C_ROPE] * qscale).astype(q_ref.dtype)
        kh = jnp.concatenate([kvf[:, hh * C_NOPE:(hh + 1) * C_NOPE], kr_pad], axis=1)
        kt_ref[hh] = kh.T[:dk].astype(kt_ref.dtype)
        v_ref[hh] = kvf[:, nn + hh * C_V:nn + (hh + 1) * C_V].astype(v_ref.dtype)


def _in_call(xl, xc, mod, g1, win, sguw, sgub, qg, wuq, kvg, wukv, cos_t, sin_t, *, nh):
    d = xl.shape[1]
    tm = TOKEN_TILE
    n_lat_tiles = xl.shape[0] // tm
    n = xl.shape[0] + xc.shape[0]
    aw = sgub.shape[1]
    bw = aw
    dk = C_NOPE + C_ROPE
    full = lambda a: pl.BlockSpec(a.shape, lambda i: (0,) * a.ndim)
    row = lambda w: pl.BlockSpec((tm, w), lambda i: (i, 0))
    kern = functools.partial(_in_kernel, n_lat_tiles=n_lat_tiles, qscale=float(dk ** -0.5 * LOG2E))
    return pl.pallas_call(
        kern,
        grid=(n // tm,),
        in_specs=[pl.BlockSpec((tm, d), lambda i: (jnp.minimum(i, n_lat_tiles - 1), 0)), full(xc),
                  full(mod), full(g1), full(win), full(sguw), full(sgub), full(qg), full(wuq), full(kvg), full(wukv),
                  row(cos_t.shape[1]), row(sin_t.shape[1])],
        out_specs=[row(aw), row(bw), row(bw),
                   pl.BlockSpec((nh, tm, dk), lambda i: (0, i, 0)),
                   pl.BlockSpec((nh, dk, tm), lambda i: (0, 0, i)),
                   pl.BlockSpec((nh, tm, C_V), lambda i: (0, i, 0))],
        out_shape=[jax.ShapeDtypeStruct((n, aw), BF16), jax.ShapeDtypeStruct((n, bw), F32), jax.ShapeDtypeStruct((n, bw), F32),
                   jax.ShapeDtypeStruct((nh, n, dk), BF16), jax.ShapeDtypeStruct((nh, dk, n), BF16),
                   jax.ShapeDtypeStruct((nh, n, C_V), BF16)],
        compiler_params=_cparams(("arbitrary",), VMEM_LIMIT),
        name="input_proj",
    )(xl, xc, mod, g1, win, sguw, sgub, qg, wuq, kvg, wukv, cos_t, sin_t)


def _tile_scan(a, b, reverse):
    n = a.shape[0]
    rows = lax.broadcasted_iota(jnp.int32, a.shape, 0)
    k = 1
    while k < n:
        sh = n - k if reverse else k
        valid = (rows < n - k) if reverse else (rows >= k)
        a_s = pltpu.roll(a, sh, 0)
        b_s = pltpu.roll(b, sh, 0)
        b = jnp.where(valid, a * b_s, 0.0) + b
        a = jnp.where(valid, a * a_s, a)
        k *= 2
    return a, b


def _softplus(x):
    return jnp.maximum(x, 0.0) + jnp.log(1.0 + jnp.exp(-jnp.abs(x)))


def _lru_kernel(*refs, reverse, nt):
    if reverse:
        xp_ref, x_ref, xn_ref, hf_ref, gb_ref, cw_ref, cb_ref, wa_ref, ba_ref, wx_ref, bx_ref, lam_ref, o_ref, carry_ref = refs
    else:
        xp_ref, x_ref, xn_ref, cw_ref, cb_ref, wa_ref, ba_ref, wx_ref, bx_ref, lam_ref, o_ref, carry_ref = refs
    tm = x_ref.shape[0]
    s = pl.program_id(0)
    tile = _lru_tile(s, nt, reverse)

    @pl.when(s == 0)
    def _():
        carry_ref[...] = jnp.zeros_like(carry_ref)

    prev_ok = jnp.logical_and(tile != 0, tile != nt - 1)
    next_ok = tile < nt - 2
    xp = jnp.where(prev_ok, xp_ref[...], 0.0)
    xn = jnp.where(next_ok, xn_ref[...], 0.0)
    ext = jnp.concatenate([xp, x_ref[...], xn], axis=0)
    cw = cw_ref[...]
    conv = cb_ref[...]
    for kk in range(CONV_W):
        st = 8 + kk - CONV_W // 2
        conv = conv + ext[st:st + tm] * cw[kk:kk + 1]

    cb16 = conv.astype(BF16)
    r = jax.nn.sigmoid(jnp.dot(cb16, wa_ref[...], preferred_element_type=F32) + ba_ref[...])
    gate = jax.nn.sigmoid(jnp.dot(cb16, wx_ref[...], preferred_element_type=F32) + bx_ref[...])
    log_a = (-LRU_C) * r * _softplus(-lam_ref[...])
    a = jnp.exp(log_a)
    mult = jnp.sqrt(jnp.maximum(-jnp.tanh(log_a) * (a * a + 1.0), 0.0))
    b = mult * (gate * conv)
    acum, bcum = _tile_scan(a, b, reverse)
    h = acum * carry_ref[...] + bcum
    carry_ref[...] = h[0:1] if reverse else h[tm - 1:tm]
    if reverse:
        o_ref[...] = ((hf_ref[...] + h) * gb_ref[...]).astype(o_ref.dtype)
    else:
        o_ref[...] = h


def _lru_tile(s, nt, reverse):
    if reverse:
        return jnp.where(s == 0, nt - 1, nt - 1 - s)
    return jnp.where(s == 0, nt - 1, s - 1)


def _lru_call(xb, hf, gb, cw, cb, wa, ba, wx, bx, lam, *, reverse):
    n, w = xb.shape
    tm = TOKEN_TILE
    nt = n // tm
    r8 = tm // 8
    tile = lambda s: _lru_tile(s, nt, reverse)
    row = pl.BlockSpec((tm, w), lambda s: (tile(s), 0))
    prev = pl.BlockSpec((8, w), lambda s: (jnp.maximum(tile(s) * r8 - 1, 0), 0))
    nxt = pl.BlockSpec((8, w), lambda s: (jnp.minimum((tile(s) + 1) * r8, n // 8 - 1), 0))
    full = lambda a: pl.BlockSpec(a.shape, lambda s: (0,) * a.ndim)
    params = [cw, cb, wa, ba, wx, bx, lam]
    ins = [xb, xb, xb] + ([hf, gb] if reverse else []) + params
    specs = [prev, row, nxt] + ([row, row] if reverse else []) + [full(p) for p in params]
    return pl.pallas_call(
        functools.partial(_lru_kernel, reverse=reverse, nt=nt),
        grid=(nt,),
        in_specs=specs,
        out_specs=row,
        out_shape=jax.ShapeDtypeStruct((n, w), BF16 if reverse else F32),
        scratch_shapes=[pltpu.VMEM((1, w), F32)],
        compiler_params=_cparams(("arbitrary",)),
        name="lru_bwd" if reverse else "lru_fwd",
    )(*ins)


def _attn_kernel(*refs, tk, nk, ts):
    q_ref, kt_ref, v_ref, o_ref = refs[:4]
    scratch = refs[4:]
    nsub = q_ref.shape[0] // ts
    m_refs, l_refs, acc_refs = scratch[:nsub], scratch[nsub:2 * nsub], scratch[2 * nsub:]
    lanes = m_refs[0].shape[1]
    nblk = tk // lanes
    for u in range(nsub):
        m_refs[u][...] = jnp.full(m_refs[u].shape, -jnp.inf, F32)
        l_refs[u][...] = jnp.zeros(l_refs[u].shape, F32)
        acc_refs[u][...] = jnp.zeros(acc_refs[u].shape, F32)

    def body(c, carry):
        off = pl.multiple_of(c * tk, tk)
        kt = kt_ref[:, pl.ds(off, tk)]
        v = v_ref[pl.ds(off, tk), :]
        for u in range(nsub):
            q = q_ref[u * ts:(u + 1) * ts, :]
            s = jnp.dot(q, kt, preferred_element_type=F32)
            m_prev = m_refs[u][...]
            m_new = jnp.maximum(m_prev, jnp.max(s, axis=-1, keepdims=True))
            alpha = jnp.exp2(m_prev - m_new)
            pb = [jnp.exp2(s[:, b * lanes:(b + 1) * lanes] - m_new) for b in range(nblk)]
            psum = pb[0]
            for b in range(1, nblk):
                psum = psum + pb[b]
            l_refs[u][...] = alpha * l_refs[u][...] + psum
            p = jnp.concatenate(pb, axis=1).astype(v.dtype)
            acc_refs[u][...] = alpha * acc_refs[u][...] + jnp.dot(p, v, preferred_element_type=F32)
            m_refs[u][...] = m_new
        return carry

    lax.fori_loop(0, nk, body, 0, unroll=min(nk, KV_UNROLL))
    for u in range(nsub):
        l = jnp.sum(l_refs[u][...], axis=-1, keepdims=True)
        o_ref[u * ts:(u + 1) * ts, :] = (acc_refs[u][...] / l).astype(o_ref.dtype)


def _attn_call(q, kt, v, *, q_row0, nq_rows, kv_row0, kv_len, tq, tk):
    nh, _, dk = q.shape
    dv = v.shape[2]
    ts = min(tq, Q_SUBTILE)
    assert q_row0 % tq == 0 and nq_rows % tq == 0 and kv_row0 % kv_len == 0 and kv_len % tk == 0 and tq % ts == 0
    assert dv == LANES and tk % LANES == 0
    nsub = tq // ts
    qb0, kvb = q_row0 // tq, kv_row0 // kv_len
    in_specs = [pl.BlockSpec((None, tq, dk), lambda h, i: (h, qb0 + i, 0)),
                pl.BlockSpec((None, dk, kv_len), lambda h, i: (h, 0, kvb)),
                pl.BlockSpec((None, kv_len, dv), lambda h, i: (h, kvb, 0))]
    return pl.pallas_call(
        functools.partial(_attn_kernel, tk=tk, nk=kv_len // tk, ts=ts),
        grid=(nh, nq_rows // tq),
        in_specs=in_specs,
        out_specs=pl.BlockSpec((tq, dv), lambda h, i: (i, h)),
        out_shape=jax.ShapeDtypeStruct((nq_rows, nh * dv), BF16),
        scratch_shapes=[pltpu.VMEM((ts, LANES), F32)] * (3 * nsub),
        compiler_params=_cparams(("arbitrary", "arbitrary"), VMEM_LIMIT),
        name="attention",
    )(q, kt, v)


def _out_kernel(xl_ref, xc_ref, ya_ref, yb_ref, ycl_ref, ycc_ref, wout_ref, mod_ref, g2_ref, wr_ref, x1_ref, h2_ref, p_ref, *,
                n_lat_tiles):
    d = xl_ref.shape[1]
    is_ctx = pl.program_id(0) >= n_lat_tiles
    x = jnp.where(is_ctx, xc_ref[...], xl_ref[...])
    yc = jnp.where(is_ctx, ycc_ref[...], ycl_ref[...])
    mod = mod_ref[...]
    m = jnp.where(is_ctx, mod[1:2], mod[0:1])
    gate1, shift2, scale2 = m[:, 2 * d:3 * d], m[:, 3 * d:4 * d], m[:, 4 * d:5 * d]
    mix = jnp.concatenate([ya_ref[...], yb_ref[...], yc], axis=-1)
    x1 = x + gate1 * jnp.dot(mix, wout_ref[...], preferred_element_type=F32)
    x1_ref[...] = x1
    h2 = _rms(x1) * g2_ref[...]
    h2 = h2 * (1.0 + scale2) + shift2
    h2_ref[...] = h2.astype(h2_ref.dtype)
    ne = p_ref.shape[1]
    h_hi = h2.astype(BF16)
    h_lo = (h2 - h_hi.astype(F32)).astype(BF16)
    wr = wr_ref[...]
    part = jnp.dot(h_hi, wr, preferred_element_type=F32)
    logits = part[:, :ne] + part[:, ne:] + jnp.dot(h_lo, wr[:, :ne], preferred_element_type=F32)
    ex = jnp.exp(logits - jnp.max(logits, axis=-1, keepdims=True))
    p_ref[...] = ex / jnp.sum(ex, axis=-1, keepdims=True)


def _out_call(xl, xc, ya, yb, ycl, ycc, wout, mod, g2, wr, *, n_tiles):
    d = xl.shape[1]
    tm = TOKEN_TILE
    n_lat_tiles = xl.shape[0] // tm
    n = n_tiles * tm
    lat = lambda w: pl.BlockSpec((tm, w), lambda i: (jnp.minimum(i, n_lat_tiles - 1), 0))
    ne = wr.shape[1] // 2
    full = lambda a: pl.BlockSpec(a.shape, lambda i: (0,) * a.ndim)
    row = lambda w: pl.BlockSpec((tm, w), lambda i: (i, 0))
    return pl.pallas_call(
        functools.partial(_out_kernel, n_lat_tiles=n_lat_tiles),
        grid=(n_tiles,),
        in_specs=[lat(d), full(xc), row(ya.shape[1]), row(yb.shape[1]), lat(ycl.shape[1]), full(ycc),
                  full(wout), full(mod), full(g2), full(wr)],
        out_specs=[row(d), row(d), row(ne)],
        out_shape=[jax.ShapeDtypeStruct((n, d), F32), jax.ShapeDtypeStruct((n, d), BF16), jax.ShapeDtypeStruct((n, ne), F32)],
        compiler_params=_cparams(("arbitrary",), VMEM_LIMIT),
        name="output_proj",
    )(xl, xc, ya, yb, ycl, ycc, wout, mod, g2, wr)


def _cumsum_lanes(x):
    n = x.shape[1]
    lane = lax.broadcasted_iota(jnp.int32, x.shape, 1)
    k = 1
    while k < n:
        x = x + jnp.where(lane >= k, pltpu.roll(x, k, 1), 0)
        k *= 2
    return x


def _count(mask):
    return jnp.sum(jnp.where(mask, 1.0, 0.0), axis=1, keepdims=True).astype(jnp.int32)


def _select_kernel(p_ref, pos_ref, excl_ref, *, cap):
    p = p_ref[...]

    def step(i, bounds):
        lo, hi = bounds
        lo_pos = jnp.maximum(lo, SMALLEST_NORMAL)
        mid = jnp.where(hi > 2.0 * lo_pos, jnp.sqrt(lo_pos) * jnp.sqrt(hi), lo + 0.5 * (hi - lo))
        mid = jnp.clip(mid, lo, hi)
        ok = _count(p >= mid) >= cap
        return jnp.where(ok, mid, lo), jnp.where(ok, hi, mid)

    rows = (p.shape[0], 1)
    lo, hi = lax.fori_loop(0, BISECT_STEPS, step, (jnp.zeros(rows, F32), jnp.full(rows, 2.0, F32)))
    gt = p >= hi
    eq = jnp.logical_and(p >= lo, p < hi)
    need = cap - _count(gt)
    eqi = jnp.where(eq, 1, 0)
    eq_rank = _cumsum_lanes(eqi) - eqi
    sel = jnp.logical_or(gt, jnp.logical_and(eq, eq_rank < need))
    seli = jnp.where(sel, 1, 0)
    excl = _cumsum_lanes(seli) - seli
    excl_ref[...] = excl
    pos_ref[...] = jnp.where(sel, excl, -1)


def _select_call(p_t, cap):
    ne, t = p_t.shape
    return pl.pallas_call(
        functools.partial(_select_kernel, cap=cap),
        out_shape=[jax.ShapeDtypeStruct((ne, t), jnp.int32), jax.ShapeDtypeStruct((ne, t), jnp.int32)],
        compiler_params=pltpu.CompilerParams(vmem_limit_bytes=VMEM_LIMIT),
        name="expert_select",
    )(p_t)


def _moe_ffn_kernel(c0_ref, pos_ref, h_ref, wg_ref, wu_ref, wd_ref, y_ref, xs_ref, *, nb, nsub, win, win_small, cap, rows):
    e = pl.program_id(0)
    j = pl.program_id(1)
    tm = TOKEN_TILE

    @pl.when(j == 0)
    def _():
        xs_ref[...] = jnp.zeros_like(xs_ref)

    base = e * (nb + 1) + j * nsub
    starts = [pl.multiple_of((c0_ref[base + u] // 8) * 8, 8) for u in range(nsub)]

    def dispatch(rows_w):
        slot = lax.broadcasted_iota(jnp.int32, (rows_w, tm), 0)
        for u in range(nsub):
            rel = pos_ref[:, u * tm:(u + 1) * tm] - starts[u]
            onehot = jnp.where(slot == rel, 1.0, 0.0).astype(BF16)
            xs_ref[pl.ds(starts[u], rows_w), :] += jnp.dot(onehot, h_ref[u * tm:(u + 1) * tm, :], preferred_element_type=F32)

    if win_small < win:
        narrow = c0_ref[base + 1] - starts[0] <= win_small
        for u in range(1, nsub):
            narrow = jnp.logical_and(narrow, c0_ref[base + u + 1] - starts[u] <= win_small)
        pl.when(narrow)(lambda: dispatch(win_small))
        pl.when(jnp.logical_not(narrow))(lambda: dispatch(win))
    else:
        dispatch(win)

    @pl.when(j == pl.num_programs(1) - 1)
    def _():
        wg = wg_ref[...].astype(BF16)
        wu = wu_ref[...].astype(BF16)
        wd = wd_ref[...].astype(BF16)
        for ci in range(cap // rows):
            xc = xs_ref[ci * rows:(ci + 1) * rows, :].astype(BF16)
            hg = jnp.dot(xc, wg, preferred_element_type=F32)
            hu = jnp.dot(xc, wu, preferred_element_type=F32)
            hid = (hg * jax.nn.sigmoid(hg) * hu).astype(BF16)
            y_ref[ci * rows:(ci + 1) * rows, :] = jnp.dot(hid, wd, preferred_element_type=F32).astype(y_ref.dtype)


def _moe_ffn_call(c0, pos_t, h2, wg, wu, wd, *, layer, cap, tile0):
    ne, t = pos_t.shape
    tm = TOKEN_TILE
    nb = t // tm
    nsub = min(nb, DISPATCH_TILES)
    gt = nsub * tm
    assert nb % nsub == 0 and tile0 % nsub == 0
    d = h2.shape[1]
    ff = wg.shape[3]
    win = min(tm, cap) + 16
    win_small = min(win, DISPATCH_NARROW)
    rows = min(cap, FFN_ROWS)
    return pl.pallas_call(
        functools.partial(_moe_ffn_kernel, nb=nb, nsub=nsub, win=win, win_small=win_small, cap=cap, rows=rows),
        grid_spec=pltpu.PrefetchScalarGridSpec(
            num_scalar_prefetch=1,
            grid=(ne, nb // nsub),
            in_specs=[pl.BlockSpec((None, 1, gt), lambda e, j, c: (e, 0, j)),
                      pl.BlockSpec((gt, d), lambda e, j, c: (tile0 // nsub + j, 0)),
                      pl.BlockSpec((None, None, d, ff), lambda e, j, c: (layer, e, 0, 0)),
                      pl.BlockSpec((None, None, d, ff), lambda e, j, c: (layer, e, 0, 0)),
                      pl.BlockSpec((None, None, ff, d), lambda e, j, c: (layer, e, 0, 0))],
            out_specs=pl.BlockSpec((None, cap, d), lambda e, j, c: (e, 0, 0)),
            scratch_shapes=[pltpu.VMEM((cap + win, d), F32)]),
        out_shape=jax.ShapeDtypeStruct((ne, cap, d), BF16),
        compiler_params=_cparams(("arbitrary", "arbitrary"), VMEM_LIMIT),
        name="moe_ffn",
    )(c0, pos_t.reshape(ne, 1, t), h2, wg, wu, wd)


def _combine_kernel(*refs, nb, nsub, kwin, kwin_small, cap, mod_row, final):
    if final:
        c0_ref, y_ref, pos_ref, p_ref, x1_ref, mod_ref, fg_ref, o_ref = refs
    else:
        c0_ref, y_ref, pos_ref, p_ref, x1_ref, mod_ref, o_ref = refs
    g = pl.program_id(0)
    e = pl.program_id(1)
    tm = TOKEN_TILE
    d = x1_ref.shape[1]
    gate2 = mod_ref[mod_row:mod_row + 1, 5 * d:6 * d]

    @pl.when(e == 0)
    def _():
        o_ref[...] = x1_ref[...]

    col = lax.broadcasted_iota(jnp.int32, (tm, pos_ref.shape[1]), 1) == e
    base = e * (nb + 1) + g * nsub

    def combine(kw):
        slot = lax.broadcasted_iota(jnp.int32, (tm, kw), 1)
        for u in range(nsub):
            w0 = pl.multiple_of(jnp.minimum((c0_ref[base + u] // 16) * 16, cap - kw), 16)
            rs = slice(u * tm, (u + 1) * tm)
            slot_of = jnp.sum(jnp.where(col, pos_ref[rs, :].astype(F32), 0.0), axis=1, keepdims=True).astype(jnp.int32)
            gate = jnp.sum(jnp.where(col, p_ref[rs, :], 0.0), axis=1, keepdims=True)
            onehot = jnp.where(slot == slot_of - w0, 1.0, 0.0).astype(BF16)
            r = jnp.dot(onehot, y_ref[pl.ds(w0, kw), :], preferred_element_type=F32)
            o_ref[rs, :] += gate * (gate2 * r)

    if kwin_small < kwin:
        narrow = True
        for u in range(nsub):
            w0s = jnp.minimum((c0_ref[base + u] // 16) * 16, cap - kwin_small)
            fits = c0_ref[base + u + 1] - w0s <= kwin_small
            narrow = fits if u == 0 else jnp.logical_and(narrow, fits)
        pl.when(narrow)(lambda: combine(kwin_small))
        pl.when(jnp.logical_not(narrow))(lambda: combine(kwin))
    else:
        combine(kwin)

    if final:
        @pl.when(e == pl.num_programs(1) - 1)
        def _():
            o_ref[...] = _rms(o_ref[...]) * fg_ref[...]


def _combine_call(c0, y, pos, probs, x1, mod, fg, *, cap, tile0, mod_row):
    ne = y.shape[0]
    t = pos.shape[0]
    tm = TOKEN_TILE
    nb = t // tm
    nsub = min(nb, COMBINE_TILES)
    gt = nsub * tm
    assert nb % nsub == 0 and tile0 % nsub == 0
    d = x1.shape[1]
    kwin = min(2 * tm, cap)
    kwin_small = min(tm, kwin)
    final = fg is not None
    full = lambda a: pl.BlockSpec(a.shape, lambda g, e, c: (0,) * a.ndim)
    in_specs = [pl.BlockSpec((None, cap, d), lambda g, e, c: (e, 0, 0)),
                pl.BlockSpec((gt, ne), lambda g, e, c: (g, 0)),
                pl.BlockSpec((gt, ne), lambda g, e, c: (g, 0)),
                pl.BlockSpec((gt, d), lambda g, e, c: (tile0 // nsub + g, 0)),
                full(mod)]
    ins = [y, pos, probs, x1, mod]
    if final:
        in_specs.append(full(fg))
        ins.append(fg)
    return pl.pallas_call(
        functools.partial(_combine_kernel, nb=nb, nsub=nsub, kwin=kwin, kwin_small=kwin_small, cap=cap, mod_row=mod_row,
                          final=final),
        grid_spec=pltpu.PrefetchScalarGridSpec(
            num_scalar_prefetch=1,
            grid=(nb // nsub, ne),
            in_specs=in_specs,
            out_specs=pl.BlockSpec((gt, d), lambda g, e, c: (g, 0))),
        out_shape=jax.ShapeDtypeStruct((t, d), F32),
        compiler_params=_cparams(("arbitrary", "arbitrary"), VMEM_LIMIT),
        name="moe_combine",
    )(c0, *ins)


def _rope_tables(t, t_ctx, nh):
    pos = jnp.arange(t)
    r = (pos // GRID_W).astype(F32)
    col = (pos % GRID_W).astype(F32)
    nf = C_ROPE // 4
    inv = ROPE_BASE ** (-jnp.arange(nf, dtype=F32) / nf)
    ar, ac = r[:, None] * inv, col[:, None] * inv
    cos_t = jnp.concatenate([jnp.cos(ar), jnp.cos(ar), jnp.cos(ac), jnp.cos(ac)], axis=-1)
    sin_t = jnp.concatenate([-jnp.sin(ar), jnp.sin(ar), -jnp.sin(ac), jnp.sin(ac)], axis=-1)
    cos_t = jnp.concatenate([cos_t, jnp.ones((t_ctx, C_ROPE), F32)], axis=0)
    sin_t = jnp.concatenate([sin_t, jnp.zeros((t_ctx, C_ROPE), F32)], axis=0)
    return jnp.tile(cos_t, (1, nh)), jnp.tile(sin_t, (1, nh))


def _swap_perm():
    nf = C_ROPE // 4
    idx = jnp.arange(C_ROPE).reshape(2, 2, nf)
    return idx[:, ::-1, :].reshape(-1)


def _moe(h2, probs_t, probs, x1, mod, fg, lp, *, layer, tile0, mod_row):
    ne, t = probs_t.shape
    cap = EC_CAPACITY * t // ne
    pos_t, excl = _select_call(probs_t, cap)
    c0 = jnp.concatenate([excl[:, ::TOKEN_TILE], jnp.full((ne, 1), cap, jnp.int32)], axis=1).reshape(-1)
    y = _moe_ffn_call(c0, pos_t, h2, lp['w_gate'], lp['w_up'], lp['w_down'], layer=layer, cap=cap, tile0=tile0)
    return _combine_call(c0, y, pos_t.T, probs, x1, mod, fg, cap=cap, tile0=tile0, mod_row=mod_row)


def kernel(x, c, ctx, c_ctx, norm1_g, w_mod, b_mod, w_in, sgu_w, sgu_b, conv_w, conv_b, lru_wa, lru_ba, lru_wx, lru_bx, lru_lambda, q_norm_g, w_uq, kv_norm_g, w_ukv, w_out, norm2_g, w_router, w_gate, w_up, w_down, final_norm_g):
    bsz, t, d = x.shape
    t_ctx = ctx.shape[1]
    depth = w_mod.shape[0]
    tm = TOKEN_TILE
    assert bsz == 1 and t_ctx == tm and t % Q_TILE == 0 and t % GRID_W == 0
    n = t + t_ctx
    nlt = t // tm
    nt = n // tm
    ql, kvl = q_norm_g.shape[1], kv_norm_g.shape[1]
    nh = w_uq.shape[2] // (C_NOPE + C_ROPE)
    aw = bw = (w_in.shape[2] - ql - kvl - C_ROPE) // 4
    assert (n % KV_CHUNK) == 0

    xl, xc = x[0], ctx[0]
    cv = jnp.zeros((8, d), F32).at[0].set(c[0]).at[1].set(c_ctx)
    mods = _mod_call(cv, w_mod, b_mod)[:, :2]
    cos_t, sin_t = _rope_tables(t, t_ctx, nh)
    perm = _swap_perm()
    row2 = lambda a: a.reshape(1, -1)

    for li in range(depth):
        last = li == depth - 1
        mod = mods[li]
        wi = w_in[li]
        o_kr = 2 * aw + 2 * bw + ql + kvl
        win = jnp.concatenate([wi, wi[:, o_kr:o_kr + C_ROPE][:, perm]], axis=1).astype(BF16)
        wq = w_uq[li].reshape(ql, nh, C_NOPE + C_ROPE)
        wq_r = wq[:, :, C_NOPE:]
        wuq = jnp.concatenate([wq[:, :, :C_NOPE].reshape(ql, -1), wq_r.reshape(ql, -1), wq_r[:, :, perm].reshape(ql, -1)],
                              axis=1).astype(BF16)
        wkv = w_ukv[li].reshape(kvl, nh, C_NOPE + C_V)
        wukv = jnp.concatenate([wkv[:, :, :C_NOPE].reshape(kvl, -1), wkv[:, :, C_NOPE:].reshape(kvl, -1)], axis=1).astype(BF16)
        sguw = sgu_w[li].transpose(1, 0, 2).reshape(CHUNK, A_GROUPS * CHUNK).astype(BF16)
        sgub = jnp.repeat(sgu_b[li].T, aw // A_GROUPS, axis=1)

        ya, xb, gb, q, kt, v = _in_call(xl, xc, mod, row2(norm1_g[li]), win, sguw, sgub, row2(q_norm_g[li]), wuq,
                                        row2(kv_norm_g[li]), wukv, cos_t, sin_t, nh=nh)

        lru = lambda dd: (conv_w[li], row2(conv_b[li]), jax.scipy.linalg.block_diag(*lru_wa[li, dd]).astype(BF16),
                          row2(lru_ba[li, dd]), jax.scipy.linalg.block_diag(*lru_wx[li, dd]).astype(BF16),
                          row2(lru_bx[li, dd]), row2(lru_lambda[li, dd]))
        hf = _lru_call(xb, None, None, *lru(0), reverse=False)
        yb = _lru_call(xb, hf, gb, *lru(1), reverse=True)

        ycl = _attn_call(q, kt, v, q_row0=0, nq_rows=t, kv_row0=0, kv_len=n, tq=Q_TILE, tk=KV_CHUNK)
        ycc = ycl[:t_ctx] if last else _attn_call(q, kt, v, q_row0=t, nq_rows=t_ctx, kv_row0=t, kv_len=t_ctx, tq=t_ctx, tk=t_ctx)

        wr_hi = w_router[li].astype(BF16)
        wr_split = jnp.concatenate([wr_hi, (w_router[li] - wr_hi.astype(F32)).astype(BF16)], axis=1)
        x1, h2, probs = _out_call(xl, xc, ya, yb, ycl, ycc, w_out[li].astype(BF16), mod, row2(norm2_g[li]), wr_split,
                                  n_tiles=nlt if last else nt)
        lp = {'w_gate': w_gate, 'w_up': w_up, 'w_down': w_down}
        probs_t = probs.T
        xl = _moe(h2, probs_t[:, :t], probs[:t], x1, mod, row2(final_norm_g) if last else None, lp, layer=li, tile0=0, mod_row=0)
        if not last:
            xc = _moe(h2, probs_t[:, t:], probs[t:], x1, mod, None, lp, layer=li, tile0=nlt, mod_row=1)
    return xl[None]
```

```python
import functools

import jax
import jax.numpy as jnp
import jax.scipy.linalg
from jax import lax
from jax.experimental import pallas as pl
from jax.experimental.pallas import tpu as pltpu

F32 = jnp.float32
BF16 = jnp.bfloat16
HIGHEST = lax.Precision.HIGHEST

GRID_W = 64
CHUNK = 128
EPS = 1e-6
A_GROUPS = 4
CONV_W = 4
LRU_C = 8.0
C_NOPE = 128
C_ROPE = 64
C_V = 128
ROPE_BASE = 10000.0
EC_CAPACITY = 2
LOG2E = 1.4426950408889634
SMALLEST_NORMAL = 1.1754943508222875e-38
BISECT_STEPS = 48

TOKEN_TILE = 256
LANES = 128
Q_TILE = 2048
Q_SUBTILE = 512
KV_CHUNK = 1280
KV_UNROLL = 2
DISPATCH_TILES = 8
DISPATCH_NARROW = 64
COMBINE_TILES = 8
COMBINE_EXPERTS = 2
FFN_ROWS = 256
VMEM_LIMIT = 60 * 1024 * 1024


def _cparams(sem, vmem=None):
    return pltpu.CompilerParams(dimension_semantics=sem, vmem_limit_bytes=vmem)


def _rms(x):
    return x * lax.rsqrt(jnp.mean(x * x, axis=-1, keepdims=True) + EPS)


def _mod_kernel(cv_ref, w_ref, b_ref, o_ref):
    cv = cv_ref[...]
    s = cv * jax.nn.sigmoid(cv)
    o_ref[...] = jnp.dot(s, w_ref[...], precision=HIGHEST, preferred_element_type=F32) + b_ref[...]


def _mod_call(cv, w_mod, b_mod):
    depth, d, d6 = w_mod.shape
    tn = 1536
    return pl.pallas_call(
        _mod_kernel,
        grid=(depth, d6 // tn),
        in_specs=[
            pl.BlockSpec((8, d), lambda l, n: (0, 0)),
            pl.BlockSpec((None, d, tn), lambda l, n: (l, 0, n)),
            pl.BlockSpec((None, 1, tn), lambda l, n: (l, 0, n)),
        ],
        out_specs=pl.BlockSpec((None, 8, tn), lambda l, n: (l, 0, n)),
        out_shape=jax.ShapeDtypeStruct((depth, 8, d6), F32),
        compiler_params=_cparams(("arbitrary", "arbitrary")),
        name="modulation",
    )(cv, w_mod, b_mod.reshape(depth, 1, d6))


def _in_kernel(xl_ref, xc_ref, mod_ref, g1_ref, win_ref, sguw_ref, sgub_ref, qg_ref, wuq_ref, kvg_ref, wukv_ref, cos_ref, sin_ref,
               ya_ref, xb_ref, gb_ref, q_ref, kt_ref, v_ref, *, n_lat_tiles, qscale):
    tm, d = xl_ref.shape
    is_ctx = pl.program_id(0) >= n_lat_tiles
    mod = mod_ref[...]
    m = jnp.where(is_ctx, mod[1:2], mod[0:1])
    shift, scale = m[:, 0:d], m[:, d:2 * d]
    h = _rms(jnp.where(is_ctx, xc_ref[...], xl_ref[...])) * g1_ref[...]
    h = h * (1.0 + scale) + shift
    z = jnp.dot(h.astype(BF16), win_ref[...], preferred_element_type=F32)

    aw = sgub_ref.shape[1]
    za = jax.nn.gelu(z[:, 0:2 * aw])
    u, val = za[:, :aw], za[:, aw:]
    vb = _rms(val).astype(BF16)
    gw = aw // A_GROUPS
    lane = lax.broadcasted_iota(jnp.int32, (CHUNK, aw), 1)
    for ci in range(tm // CHUNK):
        vc = vb[ci * CHUNK:(ci + 1) * CHUNK]
        vexp = jnp.concatenate(
            [jnp.where((lane >= g * gw) & (lane < (g + 1) * gw), vc, jnp.zeros_like(vc)) for g in range(A_GROUPS)], axis=0)
        mixed = jnp.dot(sguw_ref[...], vexp, preferred_element_type=F32) + sgub_ref[...]
        ya_ref[ci * CHUNK:(ci + 1) * CHUNK, :] = (u[ci * CHUNK:(ci + 1) * CHUNK] * mixed).astype(ya_ref.dtype)

    o = 2 * aw
    bw = xb_ref.shape[1]
    xb_ref[...] = z[:, o:o + bw]
    gb_ref[...] = jax.nn.gelu(z[:, o + bw:o + 2 * bw])
    o += 2 * bw

    ql = qg_ref.shape[1]
    kvl = kvg_ref.shape[1]
    nh = q_ref.shape[0]
    cos_t, sin_t = cos_ref[...], sin_ref[...]
    cqn = _rms(z[:, o:o + ql]) * qg_ref[...]
    qf = jnp.dot(cqn.astype(BF16), wuq_ref[...], preferred_element_type=F32)
    nn, nr = nh * C_NOPE, nh * C_ROPE
    qr = qf[:, nn:nn + nr] * cos_t + qf[:, nn + nr:nn + 2 * nr] * sin_t
    o += ql
    ckvn = _rms(z[:, o:o + kvl]) * kvg_ref[...]
    kvf = jnp.dot(ckvn.astype(BF16), wukv_ref[...], preferred_element_type=F32)
    o += kvl
    kr = z[:, o:o + C_ROPE] * cos_t[:, :C_ROPE] + z[:, o + C_ROPE:o + 2 * C_ROPE] * sin_t[:, :C_ROPE]
    dk = C_NOPE + C_ROPE
    kr_pad = jnp.concatenate([kr, jnp.zeros((tm, 2 * LANES - dk), F32)], axis=1)
    for hh in range(nh):
        q_ref[hh, :, 0:C_NOPE] = (qf[:, hh * C_NOPE:(hh + 1) * C_NOPE] * qscale).astype(q_ref.dtype)
        q_ref[hh, :, C_NOPE:dk] = (qr[:, hh * C_ROPE:(hh + 1) * C_ROPE] * qscale).astype(q_ref.dtype)
        kh = jnp.concatenate([kvf[:, hh * C_NOPE:(hh + 1) * C_NOPE], kr_pad], axis=1)
        kt_ref[hh] = kh.T[:dk].astype(kt_ref.dtype)
        v_ref[hh] = kvf[:, nn + hh * C_V:nn + (hh + 1) * C_V].astype(v_ref.dtype)


def _in_call(xl, xc, mod, g1, win, sguw, sgub, qg, wuq, kvg, wukv, cos_t, sin_t, *, nh):
    d = xl.shape[1]
    tm = TOKEN_TILE
    n_lat_tiles = xl.shape[0] // tm
    n = xl.shape[0] + xc.shape[0]
    aw = sgub.shape[1]
    bw = aw
    dk = C_NOPE + C_ROPE
    full = lambda a: pl.BlockSpec(a.shape, lambda i: (0,) * a.ndim)
    row = lambda w: pl.BlockSpec((tm, w), lambda i: (i, 0))
    kern = functools.partial(_in_kernel, n_lat_tiles=n_lat_tiles, qscale=float(dk ** -0.5 * LOG2E))
    return pl.pallas_call(
        kern,
        grid=(n // tm,),
        in_specs=[pl.BlockSpec((tm, d), lambda i: (jnp.minimum(i, n_lat_tiles - 1), 0)), full(xc),
                  full(mod), full(g1), full(win), full(sguw), full(sgub), full(qg), full(wuq), full(kvg), full(wukv),
                  row(cos_t.shape[1]), row(sin_t.shape[1])],
        out_specs=[row(aw), row(bw), row(bw),
                   pl.BlockSpec((nh, tm, dk), lambda i: (0, i, 0)),
                   pl.BlockSpec((nh, dk, tm), lambda i: (0, 0, i)),
                   pl.BlockSpec((nh, tm, C_V), lambda i: (0, i, 0))],
        out_shape=[jax.ShapeDtypeStruct((n, aw), BF16), jax.ShapeDtypeStruct((n, bw), F32), jax.ShapeDtypeStruct((n, bw), F32),
                   jax.ShapeDtypeStruct((nh, n, dk), BF16), jax.ShapeDtypeStruct((nh, dk, n), BF16),
                   jax.ShapeDtypeStruct((nh, n, C_V), BF16)],
        compiler_params=_cparams(("arbitrary",), VMEM_LIMIT),
        name="input_proj",
    )(xl, xc, mod, g1, win, sguw, sgub, qg, wuq, kvg, wukv, cos_t, sin_t)


def _tile_scan(a, b, reverse):
    n = a.shape[0]
    rows = lax.broadcasted_iota(jnp.int32, a.shape, 0)
    k = 1
    while k < n:
        sh = n - k if reverse else k
        valid = (rows < n - k) if reverse else (rows >= k)
        a_s = pltpu.roll(a, sh, 0)
        b_s = pltpu.roll(b, sh, 0)
        b = jnp.where(valid, a * b_s, 0.0) + b
        a = jnp.where(valid, a * a_s, a)
        k *= 2
    return a, b


def _softplus(x):
    return jnp.maximum(x, 0.0) + jnp.log(1.0 + jnp.exp(-jnp.abs(x)))


def _lru_kernel(*refs, reverse, nt):
    if reverse:
        xp_ref, x_ref, xn_ref, hf_ref, gb_ref, cw_ref, cb_ref, wa_ref, ba_ref, wx_ref, bx_ref, lam_ref, o_ref, carry_ref = refs
    else:
        xp_ref, x_ref, xn_ref, cw_ref, cb_ref, wa_ref, ba_ref, wx_ref, bx_ref, lam_ref, o_ref, carry_ref = refs
    tm = x_ref.shape[0]
    s = pl.program_id(0)
    tile = _lru_tile(s, nt, reverse)

    @pl.when(s == 0)
    def _():
        carry_ref[...] = jnp.zeros_like(carry_ref)

    prev_ok = jnp.logical_and(tile != 0, tile != nt - 1)
    next_ok = tile < nt - 2
    xp = jnp.where(prev_ok, xp_ref[...], 0.0)
    xn = jnp.where(next_ok, xn_ref[...], 0.0)
    ext = jnp.concatenate([xp, x_ref[...], xn], axis=0)
    cw = cw_ref[...]
    conv = cb_ref[...]
    for kk in range(CONV_W):
        st = 8 + kk - CONV_W // 2
        conv = conv + ext[st:st + tm] * cw[kk:kk + 1]

    cb16 = conv.astype(BF16)
    r = jax.nn.sigmoid(jnp.dot(cb16, wa_ref[...], preferred_element_type=F32) + ba_ref[...])
    gate = jax.nn.sigmoid(jnp.dot(cb16, wx_ref[...], preferred_element_type=F32) + bx_ref[...])
    log_a = (-LRU_C) * r * _softplus(-lam_ref[...])
    a = jnp.exp(log_a)
    mult = jnp.sqrt(jnp.maximum(-jnp.tanh(log_a) * (a * a + 1.0), 0.0))
    b = mult * (gate * conv)
    acum, bcum = _tile_scan(a, b, reverse)
    h = acum * carry_ref[...] + bcum
    carry_ref[...] = h[0:1] if reverse else h[tm - 1:tm]
    if reverse:
        o_ref[...] = ((hf_ref[...] + h) * gb_ref[...]).astype(o_ref.dtype)
    else:
        o_ref[...] = h


def _lru_tile(s, nt, reverse):
    if reverse:
        return jnp.where(s == 0, nt - 1, nt - 1 - s)
    return jnp.where(s == 0, nt - 1, s - 1)


def _lru_call(xb, hf, gb, cw, cb, wa, ba, wx, bx, lam, *, reverse):
    n, w = xb.shape
    tm = TOKEN_TILE
    nt = n // tm
    r8 = tm // 8
    tile = lambda s: _lru_tile(s, nt, reverse)
    row = pl.BlockSpec((tm, w), lambda s: (tile(s), 0))
    prev = pl.BlockSpec((8, w), lambda s: (jnp.maximum(tile(s) * r8 - 1, 0), 0))
    nxt = pl.BlockSpec((8, w), lambda s: (jnp.minimum((tile(s) + 1) * r8, n // 8 - 1), 0))
    full = lambda a: pl.BlockSpec(a.shape, lambda s: (0,) * a.ndim)
    params = [cw, cb, wa, ba, wx, bx, lam]
    ins = [xb, xb, xb] + ([hf, gb] if reverse else []) + params
    specs = [prev, row, nxt] + ([row, row] if reverse else []) + [full(p) for p in params]
    return pl.pallas_call(
        functools.partial(_lru_kernel, reverse=reverse, nt=nt),
        grid=(nt,),
        in_specs=specs,
        out_specs=row,
        out_shape=jax.ShapeDtypeStruct((n, w), BF16 if reverse else F32),
        scratch_shapes=[pltpu.VMEM((1, w), F32)],
        compiler_params=_cparams(("arbitrary",)),
        name="lru_bwd" if reverse else "lru_fwd",
    )(*ins)


def _attn_kernel(*refs, tk, nk, ts):
    q_ref, kt_ref, v_ref, o_ref = refs[:4]
    scratch = refs[4:]
    nsub = q_ref.shape[0] // ts
    m_refs, l_refs, acc_refs = scratch[:nsub], scratch[nsub:2 * nsub], scratch[2 * nsub:]
    lanes = m_refs[0].shape[1]
    nblk = tk // lanes
    for u in range(nsub):
        m_refs[u][...] = jnp.full(m_refs[u].shape, -jnp.inf, F32)
        l_refs[u][...] = jnp.zeros(l_refs[u].shape, F32)
        acc_refs[u][...] = jnp.zeros(acc_refs[u].shape, F32)

    def body(c, carry):
        off = pl.multiple_of(c * tk, tk)
        kt = kt_ref[:, pl.ds(off, tk)]
        v = v_ref[pl.ds(off, tk), :]
        for u in range(nsub):
            q = q_ref[u * ts:(u + 1) * ts, :]
            s = jnp.dot(q, kt, preferred_element_type=F32)
            m_prev = m_refs[u][...]
            m_new = jnp.maximum(m_prev, jnp.max(s, axis=-1, keepdims=True))
            alpha = jnp.exp2(m_prev - m_new)
            pb = [jnp.exp2(s[:, b * lanes:(b + 1) * lanes] - m_new) for b in range(nblk)]
            psum = pb[0]
            for b in range(1, nblk):
                psum = psum + pb[b]
            l_refs[u][...] = alpha * l_refs[u][...] + psum
            p = jnp.concatenate(pb, axis=1).astype(v.dtype)
            acc_refs[u][...] = alpha * acc_refs[u][...] + jnp.dot(p, v, preferred_element_type=F32)
            m_refs[u][...] = m_new
        return carry

    lax.fori_loop(0, nk, body, 0, unroll=min(nk, KV_UNROLL))
    for u in range(nsub):
        l = jnp.sum(l_refs[u][...], axis=-1, keepdims=True)
        o_ref[u * ts:(u + 1) * ts, :] = (acc_refs[u][...] / l).astype(o_ref.dtype)


def _attn_call(q, kt, v, *, q_row0, nq_rows, kv_row0, kv_len, tq, tk):
    nh, _, dk = q.shape
    dv = v.shape[2]
    ts = min(tq, Q_SUBTILE)
    assert q_row0 % tq == 0 and nq_rows % tq == 0 and kv_row0 % kv_len == 0 and kv_len % tk == 0 and tq % ts == 0
    assert dv == LANES and tk % LANES == 0
    nsub = tq // ts
    qb0, kvb = q_row0 // tq, kv_row0 // kv_len
    in_specs = [pl.BlockSpec((None, tq, dk), lambda h, i: (h, qb0 + i, 0)),
                pl.BlockSpec((None, dk, kv_len), lambda h, i: (h, 0, kvb)),
                pl.BlockSpec((None, kv_len, dv), lambda h, i: (h, kvb, 0))]
    return pl.pallas_call(
        functools.partial(_attn_kernel, tk=tk, nk=kv_len // tk, ts=ts),
        grid=(nh, nq_rows // tq),
        in_specs=in_specs,
        out_specs=pl.BlockSpec((tq, dv), lambda h, i: (i, h)),
        out_shape=jax.ShapeDtypeStruct((nq_rows, nh * dv), BF16),
        scratch_shapes=[pltpu.VMEM((ts, LANES), F32)] * (3 * nsub),
        compiler_params=_cparams(("arbitrary", "arbitrary"), VMEM_LIMIT),
        name="attention",
    )(q, kt, v)


def _out_kernel(xl_ref, xc_ref, ya_ref, yb_ref, ycl_ref, ycc_ref, wout_ref, mod_ref, g2_ref, wr_ref, x1_ref, h2_ref, p_ref, *,
                n_lat_tiles):
    d = xl_ref.shape[1]
    is_ctx = pl.program_id(0) >= n_lat_tiles
    x = jnp.where(is_ctx, xc_ref[...], xl_ref[...])
    yc = jnp.where(is_ctx, ycc_ref[...], ycl_ref[...])
    mod = mod_ref[...]
    m = jnp.where(is_ctx, mod[1:2], mod[0:1])
    gate1, shift2, scale2 = m[:, 2 * d:3 * d], m[:, 3 * d:4 * d], m[:, 4 * d:5 * d]
    mix = jnp.concatenate([ya_ref[...], yb_ref[...], yc], axis=-1)
    x1 = x + gate1 * jnp.dot(mix, wout_ref[...], preferred_element_type=F32)
    x1_ref[...] = x1
    h2 = _rms(x1) * g2_ref[...]
    h2 = h2 * (1.0 + scale2) + shift2
    h2_ref[...] = h2.astype(h2_ref.dtype)
    ne = p_ref.shape[1]
    h_hi = h2.astype(BF16)
    h_lo = (h2 - h_hi.astype(F32)).astype(BF16)
    wr = wr_ref[...]
    part = jnp.dot(h_hi, wr, preferred_element_type=F32)
    logits = part[:, :ne] + part[:, ne:] + jnp.dot(h_lo, wr[:, :ne], preferred_element_type=F32)
    ex = jnp.exp(logits - jnp.max(logits, axis=-1, keepdims=True))
    p_ref[...] = ex / jnp.sum(ex, axis=-1, keepdims=True)


def _out_call(xl, xc, ya, yb, ycl, ycc, wout, mod, g2, wr, *, n_tiles):
    d = xl.shape[1]
    tm = TOKEN_TILE
    n_lat_tiles = xl.shape[0] // tm
    n = n_tiles * tm
    lat = lambda w: pl.BlockSpec((tm, w), lambda i: (jnp.minimum(i, n_lat_tiles - 1), 0))
    ne = wr.shape[1] // 2
    full = lambda a: pl.BlockSpec(a.shape, lambda i: (0,) * a.ndim)
    row = lambda w: pl.BlockSpec((tm, w), lambda i: (i, 0))
    return pl.pallas_call(
        functools.partial(_out_kernel, n_lat_tiles=n_lat_tiles),
        grid=(n_tiles,),
        in_specs=[lat(d), full(xc), row(ya.shape[1]), row(yb.shape[1]), lat(ycl.shape[1]), full(ycc),
                  full(wout), full(mod), full(g2), full(wr)],
        out_specs=[row(d), row(d), row(ne)],
        out_shape=[jax.ShapeDtypeStruct((n, d), F32), jax.ShapeDtypeStruct((n, d), BF16), jax.ShapeDtypeStruct((n, ne), F32)],
        compiler_params=_cparams(("arbitrary",), VMEM_LIMIT),
        name="output_proj",
    )(xl, xc, ya, yb, ycl, ycc, wout, mod, g2, wr)


def _cumsum_lanes(x):
    n = x.shape[1]
    lane = lax.broadcasted_iota(jnp.int32, x.shape, 1)
    k = 1
    while k < n:
        x = x + jnp.where(lane >= k, pltpu.roll(x, k, 1), 0)
        k *= 2
    return x


def _count(mask):
    return jnp.sum(jnp.where(mask, 1.0, 0.0), axis=1, keepdims=True).astype(jnp.int32)


def _select_kernel(p_ref, pos_ref, excl_ref, *, cap):
    p = p_ref[...]

    def step(i, bounds):
        lo, hi = bounds
        lo_pos = jnp.maximum(lo, SMALLEST_NORMAL)
        mid = jnp.where(hi > 2.0 * lo_pos, jnp.sqrt(lo_pos) * jnp.sqrt(hi), lo + 0.5 * (hi - lo))
        mid = jnp.clip(mid, lo, hi)
        ok = _count(p >= mid) >= cap
        return jnp.where(ok, mid, lo), jnp.where(ok, hi, mid)

    rows = (p.shape[0], 1)
    lo, hi = lax.fori_loop(0, BISECT_STEPS, step, (jnp.zeros(rows, F32), jnp.full(rows, 2.0, F32)))
    gt = p >= hi
    eq = jnp.logical_and(p >= lo, p < hi)
    need = cap - _count(gt)
    eqi = jnp.where(eq, 1, 0)
    eq_rank = _cumsum_lanes(eqi) - eqi
    sel = jnp.logical_or(gt, jnp.logical_and(eq, eq_rank < need))
    seli = jnp.where(sel, 1, 0)
    excl = _cumsum_lanes(seli) - seli
    excl_ref[...] = excl
    pos_ref[...] = jnp.where(sel, excl, -1)


def _select_call(p_t, cap):
    ne, t = p_t.shape
    return pl.pallas_call(
        functools.partial(_select_kernel, cap=cap),
        out_shape=[jax.ShapeDtypeStruct((ne, t), jnp.int32), jax.ShapeDtypeStruct((ne, t), jnp.int32)],
        compiler_params=pltpu.CompilerParams(vmem_limit_bytes=VMEM_LIMIT),
        name="expert_select",
    )(p_t)


def _moe_ffn_kernel(c0_ref, pos_ref, h_ref, wg_ref, wu_ref, wd_ref, y_ref, xs_ref, *, nb, nsub, win, win_small, cap, rows):
    e = pl.program_id(0)
    j = pl.program_id(1)
    tm = TOKEN_TILE

    @pl.when(j == 0)
    def _():
        xs_ref[...] = jnp.zeros_like(xs_ref)

    base = e * (nb + 1) + j * nsub
    starts = [pl.multiple_of((c0_ref[base + u] // 8) * 8, 8) for u in range(nsub)]

    def dispatch(rows_w):
        slot = lax.broadcasted_iota(jnp.int32, (rows_w, tm), 0)
        for u in range(nsub):
            rel = pos_ref[:, u * tm:(u + 1) * tm] - starts[u]
            onehot = jnp.where(slot == rel, 1.0, 0.0).astype(BF16)
            xs_ref[pl.ds(starts[u], rows_w), :] += jnp.dot(onehot, h_ref[u * tm:(u + 1) * tm, :], preferred_element_type=F32)

    if win_small < win:
        narrow = c0_ref[base + 1] - starts[0] <= win_small
        for u in range(1, nsub):
            narrow = jnp.logical_and(narrow, c0_ref[base + u + 1] - starts[u] <= win_small)
        pl.when(narrow)(lambda: dispatch(win_small))
        pl.when(jnp.logical_not(narrow))(lambda: dispatch(win))
    else:
        dispatch(win)

    @pl.when(j == pl.num_programs(1) - 1)
    def _():
        wg = wg_ref[...].astype(BF16)
        wu = wu_ref[...].astype(BF16)
        wd = wd_ref[...].astype(BF16)
        for ci in range(cap // rows):
            xc = xs_ref[ci * rows:(ci + 1) * rows, :].astype(BF16)
            hg = jnp.dot(xc, wg, preferred_element_type=F32)
            hu = jnp.dot(xc, wu, preferred_element_type=F32)
            hid = (hg * jax.nn.sigmoid(hg) * hu).astype(BF16)
            y_ref[ci * rows:(ci + 1) * rows, :] = jnp.dot(hid, wd, preferred_element_type=F32).astype(y_ref.dtype)


def _moe_ffn_call(c0, pos_t, h2, wg, wu, wd, *, layer, cap, tile0):
    ne, t = pos_t.shape
    tm = TOKEN_TILE
    nb = t // tm
    nsub = min(nb, DISPATCH_TILES)
    gt = nsub * tm
    assert nb % nsub == 0 and tile0 % nsub == 0
    d = h2.shape[1]
    ff = wg.shape[3]
    win = min(tm, cap) + 16
    win_small = min(win, DISPATCH_NARROW)
    rows = min(cap, FFN_ROWS)
    return pl.pallas_call(
        functools.partial(_moe_ffn_kernel, nb=nb, nsub=nsub, win=win, win_small=win_small, cap=cap, rows=rows),
        grid_spec=pltpu.PrefetchScalarGridSpec(
            num_scalar_prefetch=1,
            grid=(ne, nb // nsub),
            in_specs=[pl.BlockSpec((None, 1, gt), lambda e, j, c: (e, 0, j)),
                      pl.BlockSpec((gt, d), lambda e, j, c: (tile0 // nsub + j, 0)),
                      pl.BlockSpec((None, None, d, ff), lambda e, j, c: (layer, e, 0, 0)),
                      pl.BlockSpec((None, None, d, ff), lambda e, j, c: (layer, e, 0, 0)),
                      pl.BlockSpec((None, None, ff, d), lambda e, j, c: (layer, e, 0, 0))],
            out_specs=pl.BlockSpec((None, cap, d), lambda e, j, c: (e, 0, 0)),
            scratch_shapes=[pltpu.VMEM((cap + win, d), F32)]),
        out_shape=jax.ShapeDtypeStruct((ne, cap, d), BF16),
        compiler_params=_cparams(("arbitrary", "arbitrary"), VMEM_LIMIT),
        name="moe_ffn",
    )(c0, pos_t.reshape(ne, 1, t), h2, wg, wu, wd)


def _combine_kernel(*refs, nb, nsub, nes, kwin, kwin_small, cap, mod_row, final):
    c0_ref, y_refs, refs = refs[0], refs[1:1 + nes], refs[1 + nes:]
    if final:
        pos_ref, p_ref, x1_ref, mod_ref, fg_ref, o_ref = refs
    else:
        pos_ref, p_ref, x1_ref, mod_ref, o_ref = refs
    g = pl.program_id(0)
    es = pl.program_id(1)
    tm = TOKEN_TILE
    d = x1_ref.shape[1]
    gate2 = mod_ref[mod_row:mod_row + 1, 5 * d:6 * d]

    @pl.when(es == 0)
    def _():
        o_ref[...] = x1_ref[...]

    experts = [es * nes + i for i in range(nes)]
    lane = lax.broadcasted_iota(jnp.int32, (tm, pos_ref.shape[1]), 1)
    bases = [e * (nb + 1) + g * nsub for e in experts]

    def combine(kw):
        slot = lax.broadcasted_iota(jnp.int32, (tm, kw), 1)
        for u in range(nsub):
            rs = slice(u * tm, (u + 1) * tm)
            upd = None
            for e, base, y_ref in zip(experts, bases, y_refs):
                col = lane == e
                w0 = pl.multiple_of(jnp.minimum((c0_ref[base + u] // 16) * 16, cap - kw), 16)
                slot_of = jnp.sum(jnp.where(col, pos_ref[rs, :].astype(F32), 0.0), axis=1, keepdims=True).astype(jnp.int32)
                gate = jnp.sum(jnp.where(col, p_ref[rs, :], 0.0), axis=1, keepdims=True)
                onehot = jnp.where(slot == slot_of - w0, 1.0, 0.0).astype(BF16)
                r = gate * (gate2 * jnp.dot(onehot, y_ref[pl.ds(w0, kw), :], preferred_element_type=F32))
                upd = r if upd is None else upd + r
            o_ref[rs, :] += upd

    if kwin_small < kwin:
        narrow = None
        for base in bases:
            for u in range(nsub):
                w0s = jnp.minimum((c0_ref[base + u] // 16) * 16, cap - kwin_small)
                fits = c0_ref[base + u + 1] - w0s <= kwin_small
                narrow = fits if narrow is None else jnp.logical_and(narrow, fits)
        pl.when(narrow)(lambda: combine(kwin_small))
        pl.when(jnp.logical_not(narrow))(lambda: combine(kwin))
    else:
        combine(kwin)

    if final:
        @pl.when(es == pl.num_programs(1) - 1)
        def _():
            o_ref[...] = _rms(o_ref[...]) * fg_ref[...]


def _combine_call(c0, y, pos, probs, x1, mod, fg, *, cap, tile0, mod_row):
    ne = y.shape[0]
    t = pos.shape[0]
    tm = TOKEN_TILE
    nb = t // tm
    nsub = min(nb, COMBINE_TILES)
    gt = nsub * tm
    assert nb % nsub == 0 and tile0 % nsub == 0
    d = x1.shape[1]
    kwin = min(2 * tm, cap)
    kwin_small = min(tm, kwin)
    nes = COMBINE_EXPERTS
    assert ne % nes == 0
    final = fg is not None
    full = lambda a: pl.BlockSpec(a.shape, lambda g, e, c: (0,) * a.ndim)
    in_specs = [pl.BlockSpec((None, cap, d), lambda g, e, c, i=i: (e * nes + i, 0, 0)) for i in range(nes)]
    in_specs += [pl.BlockSpec((gt, ne), lambda g, e, c: (g, 0)),
                pl.BlockSpec((gt, ne), lambda g, e, c: (g, 0)),
                pl.BlockSpec((gt, d), lambda g, e, c: (tile0 // nsub + g, 0)),
                full(mod)]
    ins = [y] * nes + [pos, probs, x1, mod]
    if final:
        in_specs.append(full(fg))
        ins.append(fg)
    return pl.pallas_call(
        functools.partial(_combine_kernel, nb=nb, nsub=nsub, nes=nes, kwin=kwin, kwin_small=kwin_small, cap=cap,
                          mod_row=mod_row, final=final),
        grid_spec=pltpu.PrefetchScalarGridSpec(
            num_scalar_prefetch=1,
            grid=(nb // nsub, ne // nes),
            in_specs=in_specs,
            out_specs=pl.BlockSpec((gt, d), lambda g, e, c: (g, 0))),
        out_shape=jax.ShapeDtypeStruct((t, d), F32),
        compiler_params=_cparams(("arbitrary", "arbitrary"), VMEM_LIMIT),
        name="moe_combine",
    )(c0, *ins)


def _rope_tables(t, t_ctx, nh):
    pos = jnp.arange(t)
    r = (pos // GRID_W).astype(F32)
    col = (pos % GRID_W).astype(F32)
    nf = C_ROPE // 4
    inv = ROPE_BASE ** (-jnp.arange(nf, dtype=F32) / nf)
    ar, ac = r[:, None] * inv, col[:, None] * inv
    cos_t = jnp.concatenate([jnp.cos(ar), jnp.cos(ar), jnp.cos(ac), jnp.cos(ac)], axis=-1)
    sin_t = jnp.concatenate([-jnp.sin(ar), jnp.sin(ar), -jnp.sin(ac), jnp.sin(ac)], axis=-1)
    cos_t = jnp.concatenate([cos_t, jnp.ones((t_ctx, C_ROPE), F32)], axis=0)
    sin_t = jnp.concatenate([sin_t, jnp.zeros((t_ctx, C_ROPE), F32)], axis=0)
    return jnp.tile(cos_t, (1, nh)), jnp.tile(sin_t, (1, nh))


def _swap_perm():
    nf = C_ROPE // 4
    idx = jnp.arange(C_ROPE).reshape(2, 2, nf)
    return idx[:, ::-1, :].reshape(-1)


def _moe(h2, probs_t, probs, x1, mod, fg, lp, *, layer, tile0, mod_row):
    ne, t = probs_t.shape
    cap = EC_CAPACITY * t // ne
    pos_t, excl = _select_call(probs_t, cap)
    c0 = jnp.concatenate([excl[:, ::TOKEN_TILE], jnp.full((ne, 1), cap, jnp.int32)], axis=1).reshape(-1)
    y = _moe_ffn_call(c0, pos_t, h2, lp['w_gate'], lp['w_up'], lp['w_down'], layer=layer, cap=cap, tile0=tile0)
    return _combine_call(c0, y, pos_t.T, probs, x1, mod, fg, cap=cap, tile0=tile0, mod_row=mod_row)


def kernel(x, c, ctx, c_ctx, norm1_g, w_mod, b_mod, w_in, sgu_w, sgu_b, conv_w, conv_b, lru_wa, lru_ba, lru_wx, lru_bx, lru_lambda, q_norm_g, w_uq, kv_norm_g, w_ukv, w_out, norm2_g, w_router, w_gate, w_up, w_down, final_norm_g):
    bsz, t, d = x.shape
    t_ctx = ctx.shape[1]
    depth = w_mod.shape[0]
    tm = TOKEN_TILE
    assert bsz == 1 and t_ctx == tm and t % Q_TILE == 0 and t % GRID_W == 0
    n = t + t_ctx
    nlt = t // tm
    nt = n // tm
    ql, kvl = q_norm_g.shape[1], kv_norm_g.shape[1]
    nh = w_uq.shape[2] // (C_NOPE + C_ROPE)
    aw = bw = (w_in.shape[2] - ql - kvl - C_ROPE) // 4
    assert (n % KV_CHUNK) == 0

    xl, xc = x[0], ctx[0]
    cv = jnp.zeros((8, d), F32).at[0].set(c[0]).at[1].set(c_ctx)
    mods = _mod_call(cv, w_mod, b_mod)[:, :2]
    cos_t, sin_t = _rope_tables(t, t_ctx, nh)
    perm = _swap_perm()
    row2 = lambda a: a.reshape(1, -1)

    for li in range(depth):
        last = li == depth - 1
        mod = mods[li]
        wi = w_in[li]
        o_kr = 2 * aw + 2 * bw + ql + kvl
        win = jnp.concatenate([wi, wi[:, o_kr:o_kr + C_ROPE][:, perm]], axis=1).astype(BF16)
        wq = w_uq[li].reshape(ql, nh, C_NOPE + C_ROPE)
        wq_r = wq[:, :, C_NOPE:]
        wuq = jnp.concatenate([wq[:, :, :C_NOPE].reshape(ql, -1), wq_r.reshape(ql, -1), wq_r[:, :, perm].reshape(ql, -1)],
                              axis=1).astype(BF16)
        wkv = w_ukv[li].reshape(kvl, nh, C_NOPE + C_V)
        wukv = jnp.concatenate([wkv[:, :, :C_NOPE].reshape(kvl, -1), wkv[:, :, C_NOPE:].reshape(kvl, -1)], axis=1).astype(BF16)
        sguw = sgu_w[li].transpose(1, 0, 2).reshape(CHUNK, A_GROUPS * CHUNK).astype(BF16)
        sgub = jnp.repeat(sgu_b[li].T, aw // A_GROUPS, axis=1)

        ya, xb, gb, q, kt, v = _in_call(xl, xc, mod, row2(norm1_g[li]), win, sguw, sgub, row2(q_norm_g[li]), wuq,
                                        row2(kv_norm_g[li]), wukv, cos_t, sin_t, nh=nh)

        lru = lambda dd: (conv_w[li], row2(conv_b[li]), jax.scipy.linalg.block_diag(*lru_wa[li, dd]).astype(BF16),
                          row2(lru_ba[li, dd]), jax.scipy.linalg.block_diag(*lru_wx[li, dd]).astype(BF16),
                          row2(lru_bx[li, dd]), row2(lru_lambda[li, dd]))
        hf = _lru_call(xb, None, None, *lru(0), reverse=False)
        yb = _lru_call(xb, hf, gb, *lru(1), reverse=True)

        ycl = _attn_call(q, kt, v, q_row0=0, nq_rows=t, kv_row0=0, kv_len=n, tq=Q_TILE, tk=KV_CHUNK)
        ycc = ycl[:t_ctx] if last else _attn_call(q, kt, v, q_row0=t, nq_rows=t_ctx, kv_row0=t, kv_len=t_ctx, tq=t_ctx, tk=t_ctx)

        wr_hi = w_router[li].astype(BF16)
        wr_split = jnp.concatenate([wr_hi, (w_router[li] - wr_hi.astype(F32)).astype(BF16)], axis=1)
        x1, h2, probs = _out_call(xl, xc, ya, yb, ycl, ycc, w_out[li].astype(BF16), mod, row2(norm2_g[li]), wr_split,
                                  n_tiles=nlt if last else nt)
        lp = {'w_gate': w_gate, 'w_up': w_up, 'w_down': w_down}
        probs_t = probs.T
        xl = _moe(h2, probs_t[:, :t], probs[:t], x1, mod, row2(final_norm_g) if last else None, lp, layer=li, tile0=0, mod_row=0)
        if not last:
            xc = _moe(h2, probs_t[:, t:], probs[t:], x1, mod, None, lp, layer=li, tile0=nlt, mod_row=1)
    return xl[None]
```

```python
import functools

import jax
import jax.numpy as jnp
import jax.scipy.linalg
from jax import lax
from jax.experimental import pallas as pl
from jax.experimental.pallas import tpu as pltpu

F32 = jnp.float32
BF16 = jnp.bfloat16
HIGHEST = lax.Precision.HIGHEST

GRID_W = 64
CHUNK = 128
EPS = 1e-6
A_GROUPS = 4
CONV_W = 4
LRU_C = 8.0
C_NOPE = 128
C_ROPE = 64
C_V = 128
ROPE_BASE = 10000.0
EC_CAPACITY = 2
LOG2E = 1.4426950408889634
SMALLEST_NORMAL = 1.1754943508222875e-38
BISECT_STEPS = 48

TOKEN_TILE = 256
LANES = 128
Q_TILE = 2048
Q_SUBTILE = 1024
KV_CHUNK = 1280
KV_UNROLL = 2
DISPATCH_TILES = 8
DISPATCH_NARROW = 64
COMBINE_TILES = 8
COMBINE_EXPERTS = 2
FFN_ROWS = 256
VMEM_LIMIT = 60 * 1024 * 1024


def _cparams(sem, vmem=None):
    return pltpu.CompilerParams(dimension_semantics=sem, vmem_limit_bytes=vmem)


def _rms(x):
    return x * lax.rsqrt(jnp.mean(x * x, axis=-1, keepdims=True) + EPS)


def _mod_kernel(cv_ref, w_ref, b_ref, o_ref):
    cv = cv_ref[...]
    s = cv * jax.nn.sigmoid(cv)
    o_ref[...] = jnp.dot(s, w_ref[...], precision=HIGHEST, preferred_element_type=F32) + b_ref[...]


def _mod_call(cv, w_mod, b_mod):
    depth, d, d6 = w_mod.shape
    tn = 1536
    return pl.pallas_call(
        _mod_kernel,
        grid=(depth, d6 // tn),
        in_specs=[
            pl.BlockSpec((8, d), lambda l, n: (0, 0)),
            pl.BlockSpec((None, d, tn), lambda l, n: (l, 0, n)),
            pl.BlockSpec((None, 1, tn), lambda l, n: (l, 0, n)),
        ],
        out_specs=pl.BlockSpec((None, 8, tn), lambda l, n: (l, 0, n)),
        out_shape=jax.ShapeDtypeStruct((depth, 8, d6), F32),
        compiler_params=_cparams(("arbitrary", "arbitrary")),
        name="modulation",
    )(cv, w_mod, b_mod.reshape(depth, 1, d6))


def _in_kernel(xl_ref, xc_ref, mod_ref, g1_ref, win_ref, sguw_ref, sgub_ref, qg_ref, wuq_ref, kvg_ref, wukv_ref, cos_ref, sin_ref,
               ya_ref, xb_ref, gb_ref, q_ref, kt_ref, v_ref, *, n_lat_tiles, qscale):
    tm, d = xl_ref.shape
    is_ctx = pl.program_id(0) >= n_lat_tiles
    mod = mod_ref[...]
    m = jnp.where(is_ctx, mod[1:2], mod[0:1])
    shift, scale = m[:, 0:d], m[:, d:2 * d]
    h = _rms(jnp.where(is_ctx, xc_ref[...], xl_ref[...])) * g1_ref[...]
    h = h * (1.0 + scale) + shift
    z = jnp.dot(h.astype(BF16), win_ref[...], preferred_element_type=F32)

    aw = sgub_ref.shape[1]
    za = jax.nn.gelu(z[:, 0:2 * aw])
    u, val = za[:, :aw], za[:, aw:]
    vb = _rms(val).astype(BF16)
    gw = aw // A_GROUPS
    lane = lax.broadcasted_iota(jnp.int32, (CHUNK, aw), 1)
    for ci in range(tm // CHUNK):
        vc = vb[ci * CHUNK:(ci + 1) * CHUNK]
        vexp = jnp.concatenate(
            [jnp.where((lane >= g * gw) & (lane < (g + 1) * gw), vc, jnp.zeros_like(vc)) for g in range(A_GROUPS)], axis=0)
        mixed = jnp.dot(sguw_ref[...], vexp, preferred_element_type=F32) + sgub_ref[...]
        ya_ref[ci * CHUNK:(ci + 1) * CHUNK, :] = (u[ci * CHUNK:(ci + 1) * CHUNK] * mixed).astype(ya_ref.dtype)

    o = 2 * aw
    bw = xb_ref.shape[1]
    xb_ref[...] = z[:, o:o + bw]
    gb_ref[...] = jax.nn.gelu(z[:, o + bw:o + 2 * bw])
    o += 2 * bw

    ql = qg_ref.shape[1]
    kvl = kvg_ref.shape[1]
    nh = q_ref.shape[0]
    cos_t, sin_t = cos_ref[...], sin_ref[...]
    cqn = _rms(z[:, o:o + ql]) * qg_ref[...]
    qf = jnp.dot(cqn.astype(BF16), wuq_ref[...], preferred_element_type=F32)
    nn, nr = nh * C_NOPE, nh * C_ROPE
    qr = qf[:, nn:nn + nr] * cos_t + qf[:, nn + nr:nn + 2 * nr] * sin_t
    o += ql
    ckvn = _rms(z[:, o:o + kvl]) * kvg_ref[...]
    kvf = jnp.dot(ckvn.astype(BF16), wukv_ref[...], preferred_element_type=F32)
    o += kvl
    kr = z[:, o:o + C_ROPE] * cos_t[:, :C_ROPE] + z[:, o + C_ROPE:o + 2 * C_ROPE] * sin_t[:, :C_ROPE]
    dk = C_NOPE + C_ROPE
    kr_pad = jnp.concatenate([kr, jnp.zeros((tm, 2 * LANES - dk), F32)], axis=1)
    for hh in range(nh):
        q_ref[hh, :, 0:C_NOPE] = (qf[:, hh * C_NOPE:(hh + 1) * C_NOPE] * qscale).astype(q_ref.dtype)
        q_ref[hh, :, C_NOPE:dk] = (qr[:, hh * C_ROPE:(hh + 1) * C_ROPE] * qscale).astype(q_ref.dtype)
        kh = jnp.concatenate([kvf[:, hh * C_NOPE:(hh + 1) * C_NOPE], kr_pad], axis=1)
        kt_ref[hh] = kh.T[:dk].astype(kt_ref.dtype)
        v_ref[hh] = kvf[:, nn + hh * C_V:nn + (hh + 1) * C_V].astype(v_ref.dtype)


def _in_call(xl, xc, mod, g1, win, sguw, sgub, qg, wuq, kvg, wukv, cos_t, sin_t, *, nh):
    d = xl.shape[1]
    tm = TOKEN_TILE
    n_lat_tiles = xl.shape[0] // tm
    n = xl.shape[0] + xc.shape[0]
    aw = sgub.shape[1]
    bw = aw
    dk = C_NOPE + C_ROPE
    full = lambda a: pl.BlockSpec(a.shape, lambda i: (0,) * a.ndim)
    row = lambda w: pl.BlockSpec((tm, w), lambda i: (i, 0))
    kern = functools.partial(_in_kernel, n_lat_tiles=n_lat_tiles, qscale=float(dk ** -0.5 * LOG2E))
    return pl.pallas_call(
        kern,
        grid=(n // tm,),
        in_specs=[pl.BlockSpec((tm, d), lambda i: (jnp.minimum(i, n_lat_tiles - 1), 0)), full(xc),
                  full(mod), full(g1), full(win), full(sguw), full(sgub), full(qg), full(wuq), full(kvg), full(wukv),
                  row(cos_t.shape[1]), row(sin_t.shape[1])],
        out_specs=[row(aw), row(bw), row(bw),
                   pl.BlockSpec((nh, tm, dk), lambda i: (0, i, 0)),
                   pl.BlockSpec((nh, dk, tm), lambda i: (0, 0, i)),
                   pl.BlockSpec((nh, tm, C_V), lambda i: (0, i, 0))],
        out_shape=[jax.ShapeDtypeStruct((n, aw), BF16), jax.ShapeDtypeStruct((n, bw), F32), jax.ShapeDtypeStruct((n, bw), F32),
                   jax.ShapeDtypeStruct((nh, n, dk), BF16), jax.ShapeDtypeStruct((nh, dk, n), BF16),
                   jax.ShapeDtypeStruct((nh, n, C_V), BF16)],
        compiler_params=_cparams(("arbitrary",), VMEM_LIMIT),
        name="input_proj",
    )(xl, xc, mod, g1, win, sguw, sgub, qg, wuq, kvg, wukv, cos_t, sin_t)


def _tile_scan(a, b, reverse):
    n = a.shape[0]
    rows = lax.broadcasted_iota(jnp.int32, a.shape, 0)
    k = 1
    while k < n:
        sh = n - k if reverse else k
        valid = (rows < n - k) if reverse else (rows >= k)
        a_s = pltpu.roll(a, sh, 0)
        b_s = pltpu.roll(b, sh, 0)
        b = jnp.where(valid, a * b_s, 0.0) + b
        a = jnp.where(valid, a * a_s, a)
        k *= 2
    return a, b


def _softplus(x):
    return jnp.maximum(x, 0.0) + jnp.log(1.0 + jnp.exp(-jnp.abs(x)))


def _lru_kernel(*refs, reverse, nt):
    if reverse:
        xp_ref, x_ref, xn_ref, hf_ref, gb_ref, cw_ref, cb_ref, wa_ref, ba_ref, wx_ref, bx_ref, lam_ref, o_ref, carry_ref = refs
    else:
        xp_ref, x_ref, xn_ref, cw_ref, cb_ref, wa_ref, ba_ref, wx_ref, bx_ref, lam_ref, o_ref, carry_ref = refs
    tm = x_ref.shape[0]
    s = pl.program_id(0)
    tile = _lru_tile(s, nt, reverse)

    @pl.when(s == 0)
    def _():
        carry_ref[...] = jnp.zeros_like(carry_ref)

    prev_ok = jnp.logical_and(tile != 0, tile != nt - 1)
    next_ok = tile < nt - 2
    xp = jnp.where(prev_ok, xp_ref[...], 0.0)
    xn = jnp.where(next_ok, xn_ref[...], 0.0)
    ext = jnp.concatenate([xp, x_ref[...], xn], axis=0)
    cw = cw_ref[...]
    conv = cb_ref[...]
    for kk in range(CONV_W):
        st = 8 + kk - CONV_W // 2
        conv = conv + ext[st:st + tm] * cw[kk:kk + 1]

    cb16 = conv.astype(BF16)
    r = jax.nn.sigmoid(jnp.dot(cb16, wa_ref[...], preferred_element_type=F32) + ba_ref[...])
    gate = jax.nn.sigmoid(jnp.dot(cb16, wx_ref[...], preferred_element_type=F32) + bx_ref[...])
    log_a = (-LRU_C) * r * _softplus(-lam_ref[...])
    a = jnp.exp(log_a)
    mult = jnp.sqrt(jnp.maximum(-jnp.tanh(log_a) * (a * a + 1.0), 0.0))
    b = mult * (gate * conv)
    acum, bcum = _tile_scan(a, b, reverse)
    h = acum * carry_ref[...] + bcum
    carry_ref[...] = h[0:1] if reverse else h[tm - 1:tm]
    if reverse:
        o_ref[...] = ((hf_ref[...] + h) * gb_ref[...]).astype(o_ref.dtype)
    else:
        o_ref[...] = h


def _lru_tile(s, nt, reverse):
    if reverse:
        return jnp.where(s == 0, nt - 1, nt - 1 - s)
    return jnp.where(s == 0, nt - 1, s - 1)


def _lru_call(xb, hf, gb, cw, cb, wa, ba, wx, bx, lam, *, reverse):
    n, w = xb.shape
    tm = TOKEN_TILE
    nt = n // tm
    r8 = tm // 8
    tile = lambda s: _lru_tile(s, nt, reverse)
    row = pl.BlockSpec((tm, w), lambda s: (tile(s), 0))
    prev = pl.BlockSpec((8, w), lambda s: (jnp.maximum(tile(s) * r8 - 1, 0), 0))
    nxt = pl.BlockSpec((8, w), lambda s: (jnp.minimum((tile(s) + 1) * r8, n // 8 - 1), 0))
    full = lambda a: pl.BlockSpec(a.shape, lambda s: (0,) * a.ndim)
    params = [cw, cb, wa, ba, wx, bx, lam]
    ins = [xb, xb, xb] + ([hf, gb] if reverse else []) + params
    specs = [prev, row, nxt] + ([row, row] if reverse else []) + [full(p) for p in params]
    return pl.pallas_call(
        functools.partial(_lru_kernel, reverse=reverse, nt=nt),
        grid=(nt,),
        in_specs=specs,
        out_specs=row,
        out_shape=jax.ShapeDtypeStruct((n, w), BF16 if reverse else F32),
        scratch_shapes=[pltpu.VMEM((1, w), F32)],
        compiler_params=_cparams(("arbitrary",)),
        name="lru_bwd" if reverse else "lru_fwd",
    )(*ins)


def _attn_kernel(*refs, tk, nk, ts):
    q_ref, kt_ref, v_ref, o_ref = refs[:4]
    scratch = refs[4:]
    nsub = q_ref.shape[0] // ts
    m_refs, l_refs, acc_refs = scratch[:nsub], scratch[nsub:2 * nsub], scratch[2 * nsub:]
    lanes = m_refs[0].shape[1]
    nblk = tk // lanes
    for u in range(nsub):
        m_refs[u][...] = jnp.full(m_refs[u].shape, -jnp.inf, F32)
        l_refs[u][...] = jnp.zeros(l_refs[u].shape, F32)
        acc_refs[u][...] = jnp.zeros(acc_refs[u].shape, F32)

    def body(c, carry):
        off = pl.multiple_of(c * tk, tk)
        kt = kt_ref[:, pl.ds(off, tk)]
        v = v_ref[pl.ds(off, tk), :]
        for u in range(nsub):
            q = q_ref[u * ts:(u + 1) * ts, :]
            s = jnp.dot(q, kt, preferred_element_type=F32)
            m_prev = m_refs[u][...]
            m_new = jnp.maximum(m_prev, jnp.max(s, axis=-1, keepdims=True))
            alpha = jnp.exp2(m_prev - m_new)
            pb = [jnp.exp2(s[:, b * lanes:(b + 1) * lanes] - m_new) for b in range(nblk)]
            psum = pb[0]
            for b in range(1, nblk):
                psum = psum + pb[b]
            l_refs[u][...] = alpha * l_refs[u][...] + psum
            p = jnp.concatenate(pb, axis=1).astype(v.dtype)
            acc_refs[u][...] = alpha * acc_refs[u][...] + jnp.dot(p, v, preferred_element_type=F32)
            m_refs[u][...] = m_new
        return carry

    lax.fori_loop(0, nk, body, 0, unroll=min(nk, KV_UNROLL))
    for u in range(nsub):
        l = jnp.sum(l_refs[u][...], axis=-1, keepdims=True)
        o_ref[u * ts:(u + 1) * ts, :] = (acc_refs[u][...] / l).astype(o_ref.dtype)


def _attn_call(q, kt, v, *, q_row0, nq_rows, kv_row0, kv_len, tq, tk):
    nh, _, dk = q.shape
    dv = v.shape[2]
    ts = min(tq, Q_SUBTILE)
    assert q_row0 % tq == 0 and nq_rows % tq == 0 and kv_row0 % kv_len == 0 and kv_len % tk == 0 and tq % ts == 0
    assert dv == LANES and tk % LANES == 0
    nsub = tq // ts
    qb0, kvb = q_row0 // tq, kv_row0 // kv_len
    in_specs = [pl.BlockSpec((None, tq, dk), lambda h, i: (h, qb0 + i, 0)),
                pl.BlockSpec((None, dk, kv_len), lambda h, i: (h, 0, kvb)),
                pl.BlockSpec((None, kv_len, dv), lambda h, i: (h, kvb, 0))]
    return pl.pallas_call(
        functools.partial(_attn_kernel, tk=tk, nk=kv_len // tk, ts=ts),
        grid=(nh, nq_rows // tq),
        in_specs=in_specs,
        out_specs=pl.BlockSpec((tq, dv), lambda h, i: (i, h)),
        out_shape=jax.ShapeDtypeStruct((nq_rows, nh * dv), BF16),
        scratch_shapes=[pltpu.VMEM((ts, LANES), F32)] * (3 * nsub),
        compiler_params=_cparams(("arbitrary", "arbitrary"), VMEM_LIMIT),
        name="attention",
    )(q, kt, v)


def _out_kernel(xl_ref, xc_ref, ya_ref, yb_ref, ycl_ref, ycc_ref, wout_ref, mod_ref, g2_ref, wr_ref, x1_ref, h2_ref, p_ref, *,
                n_lat_tiles):
    d = xl_ref.shape[1]
    is_ctx = pl.program_id(0) >= n_lat_tiles
    x = jnp.where(is_ctx, xc_ref[...], xl_ref[...])
    yc = jnp.where(is_ctx, ycc_ref[...], ycl_ref[...])
    mod = mod_ref[...]
    m = jnp.where(is_ctx, mod[1:2], mod[0:1])
    gate1, shift2, scale2 = m[:, 2 * d:3 * d], m[:, 3 * d:4 * d], m[:, 4 * d:5 * d]
    mix = jnp.concatenate([ya_ref[...], yb_ref[...], yc], axis=-1)
    x1 = x + gate1 * jnp.dot(mix, wout_ref[...], preferred_element_type=F32)
    x1_ref[...] = x1
    h2 = _rms(x1) * g2_ref[...]
    h2 = h2 * (1.0 + scale2) + shift2
    h2_ref[...] = h2.astype(h2_ref.dtype)
    ne = p_ref.shape[1]
    h_hi = h2.astype(BF16)
    h_lo = (h2 - h_hi.astype(F32)).astype(BF16)
    wr = wr_ref[...]
    part = jnp.dot(h_hi, wr, preferred_element_type=F32)
    logits = part[:, :ne] + part[:, ne:] + jnp.dot(h_lo, wr[:, :ne], preferred_element_type=F32)
    ex = jnp.exp(logits - jnp.max(logits, axis=-1, keepdims=True))
    p_ref[...] = ex / jnp.sum(ex, axis=-1, keepdims=True)


def _out_call(xl, xc, ya, yb, ycl, ycc, wout, mod, g2, wr, *, n_tiles):
    d = xl.shape[1]
    tm = TOKEN_TILE
    n_lat_tiles = xl.shape[0] // tm
    n = n_tiles * tm
    lat = lambda w: pl.BlockSpec((tm, w), lambda i: (jnp.minimum(i, n_lat_tiles - 1), 0))
    ne = wr.shape[1] // 2
    full = lambda a: pl.BlockSpec(a.shape, lambda i: (0,) * a.ndim)
    row = lambda w: pl.BlockSpec((tm, w), lambda i: (i, 0))
    return pl.pallas_call(
        functools.partial(_out_kernel, n_lat_tiles=n_lat_tiles),
        grid=(n_tiles,),
        in_specs=[lat(d), full(xc), row(ya.shape[1]), row(yb.shape[1]), lat(ycl.shape[1]), full(ycc),
                  full(wout), full(mod), full(g2), full(wr)],
        out_specs=[row(d), row(d), row(ne)],
        out_shape=[jax.ShapeDtypeStruct((n, d), F32), jax.ShapeDtypeStruct((n, d), BF16), jax.ShapeDtypeStruct((n, ne), F32)],
        compiler_params=_cparams(("arbitrary",), VMEM_LIMIT),
        name="output_proj",
    )(xl, xc, ya, yb, ycl, ycc, wout, mod, g2, wr)


def _cumsum_lanes(x):
    n = x.shape[1]
    lane = lax.broadcasted_iota(jnp.int32, x.shape, 1)
    k = 1
    while k < n:
        x = x + jnp.where(lane >= k, pltpu.roll(x, k, 1), 0)
        k *= 2
    return x


def _count(mask):
    return jnp.sum(jnp.where(mask, 1.0, 0.0), axis=1, keepdims=True).astype(jnp.int32)


def _select_kernel(p_ref, pos_ref, excl_ref, *, cap):
    p = p_ref[...]

    def step(i, bounds):
        lo, hi = bounds
        lo_pos = jnp.maximum(lo, SMALLEST_NORMAL)
        mid = jnp.where(hi > 2.0 * lo_pos, jnp.sqrt(lo_pos) * jnp.sqrt(hi), lo + 0.5 * (hi - lo))
        mid = jnp.clip(mid, lo, hi)
        ok = _count(p >= mid) >= cap
        return jnp.where(ok, mid, lo), jnp.where(ok, hi, mid)

    rows = (p.shape[0], 1)
    lo, hi = lax.fori_loop(0, BISECT_STEPS, step, (jnp.zeros(rows, F32), jnp.full(rows, 2.0, F32)))
    gt = p >= hi
    eq = jnp.logical_and(p >= lo, p < hi)
    need = cap - _count(gt)
    eqi = jnp.where(eq, 1, 0)
    eq_rank = _cumsum_lanes(eqi) - eqi
    sel = jnp.logical_or(gt, jnp.logical_and(eq, eq_rank < need))
    seli = jnp.where(sel, 1, 0)
    excl = _cumsum_lanes(seli) - seli
    excl_ref[...] = excl
    pos_ref[...] = jnp.where(sel, excl, -1)


def _select_call(p_t, cap):
    ne, t = p_t.shape
    return pl.pallas_call(
        functools.partial(_select_kernel, cap=cap),
        out_shape=[jax.ShapeDtypeStruct((ne, t), jnp.int32), jax.ShapeDtypeStruct((ne, t), jnp.int32)],
        compiler_params=pltpu.CompilerParams(vmem_limit_bytes=VMEM_LIMIT),
        name="expert_select",
    )(p_t)


def _moe_ffn_kernel(c0_ref, pos_ref, h_ref, wg_ref, wu_ref, wd_ref, y_ref, xs_ref, *, nb, nsub, win, win_small, cap, rows):
    e = pl.program_id(0)
    j = pl.program_id(1)
    tm = TOKEN_TILE

    @pl.when(j == 0)
    def _():
        xs_ref[...] = jnp.zeros_like(xs_ref)

    base = e * (nb + 1) + j * nsub
    starts = [pl.multiple_of((c0_ref[base + u] // 8) * 8, 8) for u in range(nsub)]

    def dispatch(rows_w):
        slot = lax.broadcasted_iota(jnp.int32, (rows_w, tm), 0)
        for u in range(nsub):
            rel = pos_ref[:, u * tm:(u + 1) * tm] - starts[u]
            onehot = jnp.where(slot == rel, 1.0, 0.0).astype(BF16)
            xs_ref[pl.ds(starts[u], rows_w), :] += jnp.dot(onehot, h_ref[u * tm:(u + 1) * tm, :], preferred_element_type=F32)

    if win_small < win:
        narrow = c0_ref[base + 1] - starts[0] <= win_small
        for u in range(1, nsub):
            narrow = jnp.logical_and(narrow, c0_ref[base + u + 1] - starts[u] <= win_small)
        pl.when(narrow)(lambda: dispatch(win_small))
        pl.when(jnp.logical_not(narrow))(lambda: dispatch(win))
    else:
        dispatch(win)

    @pl.when(j == pl.num_programs(1) - 1)
    def _():
        wg = wg_ref[...].astype(BF16)
        wu = wu_ref[...].astype(BF16)
        wd = wd_ref[...].astype(BF16)
        for ci in range(cap // rows):
            xc = xs_ref[ci * rows:(ci + 1) * rows, :].astype(BF16)
            hg = jnp.dot(xc, wg, preferred_element_type=F32)
            hu = jnp.dot(xc, wu, preferred_element_type=F32)
            hid = (hg * jax.nn.sigmoid(hg) * hu).astype(BF16)
            y_ref[ci * rows:(ci + 1) * rows, :] = jnp.dot(hid, wd, preferred_element_type=F32).astype(y_ref.dtype)


def _moe_ffn_call(c0, pos_t, h2, wg, wu, wd, *, layer, cap, tile0):
    ne, t = pos_t.shape
    tm = TOKEN_TILE
    nb = t // tm
    nsub = min(nb, DISPATCH_TILES)
    gt = nsub * tm
    assert nb % nsub == 0 and tile0 % nsub == 0
    d = h2.shape[1]
    ff = wg.shape[3]
    win = min(tm, cap) + 16
    win_small = min(win, DISPATCH_NARROW)
    rows = min(cap, FFN_ROWS)
    return pl.pallas_call(
        functools.partial(_moe_ffn_kernel, nb=nb, nsub=nsub, win=win, win_small=win_small, cap=cap, rows=rows),
        grid_spec=pltpu.PrefetchScalarGridSpec(
            num_scalar_prefetch=1,
            grid=(ne, nb // nsub),
            in_specs=[pl.BlockSpec((None, 1, gt), lambda e, j, c: (e, 0, j)),
                      pl.BlockSpec((gt, d), lambda e, j, c: (tile0 // nsub + j, 0)),
                      pl.BlockSpec((None, None, d, ff), lambda e, j, c: (layer, e, 0, 0)),
                      pl.BlockSpec((None, None, d, ff), lambda e, j, c: (layer, e, 0, 0)),
                      pl.BlockSpec((None, None, ff, d), lambda e, j, c: (layer, e, 0, 0))],
            out_specs=pl.BlockSpec((None, cap, d), lambda e, j, c: (e, 0, 0)),
            scratch_shapes=[pltpu.VMEM((cap + win, d), F32)]),
        out_shape=jax.ShapeDtypeStruct((ne, cap, d), BF16),
        compiler_params=_cparams(("arbitrary", "arbitrary"), VMEM_LIMIT),
        name="moe_ffn",
    )(c0, pos_t.reshape(ne, 1, t), h2, wg, wu, wd)


def _combine_kernel(*refs, nb, nsub, nes, kwin, kwin_small, cap, mod_row, final):
    c0_ref, y_refs, refs = refs[0], refs[1:1 + nes], refs[1 + nes:]
    if final:
        pos_ref, p_ref, x1_ref, mod_ref, fg_ref, o_ref = refs
    else:
        pos_ref, p_ref, x1_ref, mod_ref, o_ref = refs
    g = pl.program_id(0)
    es = pl.program_id(1)
    tm = TOKEN_TILE
    d = x1_ref.shape[1]
    gate2 = mod_ref[mod_row:mod_row + 1, 5 * d:6 * d]

    @pl.when(es == 0)
    def _():
        o_ref[...] = x1_ref[...]

    experts = [es * nes + i for i in range(nes)]
    lane = lax.broadcasted_iota(jnp.int32, (tm, pos_ref.shape[1]), 1)
    bases = [e * (nb + 1) + g * nsub for e in experts]

    def combine(kw):
        slot = lax.broadcasted_iota(jnp.int32, (tm, kw), 1)
        for u in range(nsub):
            rs = slice(u * tm, (u + 1) * tm)
            upd = None
            for e, base, y_ref in zip(experts, bases, y_refs):
                col = lane == e
                w0 = pl.multiple_of(jnp.minimum((c0_ref[base + u] // 16) * 16, cap - kw), 16)
                slot_of = jnp.sum(jnp.where(col, pos_ref[rs, :].astype(F32), 0.0), axis=1, keepdims=True).astype(jnp.int32)
                gate = jnp.sum(jnp.where(col, p_ref[rs, :], 0.0), axis=1, keepdims=True)
                onehot = jnp.where(slot == slot_of - w0, 1.0, 0.0).astype(BF16)
                r = gate * (gate2 * jnp.dot(onehot, y_ref[pl.ds(w0, kw), :], preferred_element_type=F32))
                upd = r if upd is None else upd + r
            o_ref[rs, :] += upd

    if kwin_small < kwin:
        narrow = None
        for base in bases:
            for u in range(nsub):
                w0s = jnp.minimum((c0_ref[base + u] // 16) * 16, cap - kwin_small)
                fits = c0_ref[base + u + 1] - w0s <= kwin_small
                narrow = fits if narrow is None else jnp.logical_and(narrow, fits)
        pl.when(narrow)(lambda: combine(kwin_small))
        pl.when(jnp.logical_not(narrow))(lambda: combine(kwin))
    else:
        combine(kwin)

    if final:
        @pl.when(es == pl.num_programs(1) - 1)
        def _():
            o_ref[...] = _rms(o_ref[...]) * fg_ref[...]


def _combine_call(c0, y, pos, probs, x1, mod, fg, *, cap, tile0, mod_row):
    ne = y.shape[0]
    t = pos.shape[0]
    tm = TOKEN_TILE
    nb = t // tm
    nsub = min(nb, COMBINE_TILES)
    gt = nsub * tm
    assert nb % nsub == 0 and tile0 % nsub == 0
    d = x1.shape[1]
    kwin = min(2 * tm, cap)
    kwin_small = min(tm, kwin)
    nes = COMBINE_EXPERTS
    assert ne % nes == 0
    final = fg is not None
    full = lambda a: pl.BlockSpec(a.shape, lambda g, e, c: (0,) * a.ndim)
    in_specs = [pl.BlockSpec((None, cap, d), lambda g, e, c, i=i: (e * nes + i, 0, 0)) for i in range(nes)]
    in_specs += [pl.BlockSpec((gt, ne), lambda g, e, c: (g, 0)),
                pl.BlockSpec((gt, ne), lambda g, e, c: (g, 0)),
                pl.BlockSpec((gt, d), lambda g, e, c: (tile0 // nsub + g, 0)),
                full(mod)]
    ins = [y] * nes + [pos, probs, x1, mod]
    if final:
        in_specs.append(full(fg))
        ins.append(fg)
    return pl.pallas_call(
        functools.partial(_combine_kernel, nb=nb, nsub=nsub, nes=nes, kwin=kwin, kwin_small=kwin_small, cap=cap,
                          mod_row=mod_row, final=final),
        grid_spec=pltpu.PrefetchScalarGridSpec(
            num_scalar_prefetch=1,
            grid=(nb // nsub, ne // nes),
            in_specs=in_specs,
            out_specs=pl.BlockSpec((gt, d), lambda g, e, c: (g, 0))),
        out_shape=jax.ShapeDtypeStruct((t, d), F32),
        compiler_params=_cparams(("arbitrary", "arbitrary"), VMEM_LIMIT),
        name="moe_combine",
    )(c0, *ins)


def _rope_tables(t, t_ctx, nh):
    pos = jnp.arange(t)
    r = (pos // GRID_W).astype(F32)
    col = (pos % GRID_W).astype(F32)
    nf = C_ROPE // 4
    inv = ROPE_BASE ** (-jnp.arange(nf, dtype=F32) / nf)
    ar, ac = r[:, None] * inv, col[:, None] * inv
    cos_t = jnp.concatenate([jnp.cos(ar), jnp.cos(ar), jnp.cos(ac), jnp.cos(ac)], axis=-1)
    sin_t = jnp.concatenate([-jnp.sin(ar), jnp.sin(ar), -jnp.sin(ac), jnp.sin(ac)], axis=-1)
    cos_t = jnp.concatenate([cos_t, jnp.ones((t_ctx, C_ROPE), F32)], axis=0)
    sin_t = jnp.concatenate([sin_t, jnp.zeros((t_ctx, C_ROPE), F32)], axis=0)
    return jnp.tile(cos_t, (1, nh)), jnp.tile(sin_t, (1, nh))


def _swap_perm():
    nf = C_ROPE // 4
    idx = jnp.arange(C_ROPE).reshape(2, 2, nf)
    return idx[:, ::-1, :].reshape(-1)


def _moe(h2, probs_t, probs, x1, mod, fg, lp, *, layer, tile0, mod_row):
    ne, t = probs_t.shape
    cap = EC_CAPACITY * t // ne
    pos_t, excl = _select_call(probs_t, cap)
    c0 = jnp.concatenate([excl[:, ::TOKEN_TILE], jnp.full((ne, 1), cap, jnp.int32)], axis=1).reshape(-1)
    y = _moe_ffn_call(c0, pos_t, h2, lp['w_gate'], lp['w_up'], lp['w_down'], layer=layer, cap=cap, tile0=tile0)
    return _combine_call(c0, y, pos_t.T, probs, x1, mod, fg, cap=cap, tile0=tile0, mod_row=mod_row)


def kernel(x, c, ctx, c_ctx, norm1_g, w_mod, b_mod, w_in, sgu_w, sgu_b, conv_w, conv_b, lru_wa, lru_ba, lru_wx, lru_bx, lru_lambda, q_norm_g, w_uq, kv_norm_g, w_ukv, w_out, norm2_g, w_router, w_gate, w_up, w_down, final_norm_g):
    bsz, t, d = x.shape
    t_ctx = ctx.shape[1]
    depth = w_mod.shape[0]
    tm = TOKEN_TILE
    assert bsz == 1 and t_ctx == tm and t % Q_TILE == 0 and t % GRID_W == 0
    n = t + t_ctx
    nlt = t // tm
    nt = n // tm
    ql, kvl = q_norm_g.shape[1], kv_norm_g.shape[1]
    nh = w_uq.shape[2] // (C_NOPE + C_ROPE)
    aw = bw = (w_in.shape[2] - ql - kvl - C_ROPE) // 4
    assert (n % KV_CHUNK) == 0

    xl, xc = x[0], ctx[0]
    cv = jnp.zeros((8, d), F32).at[0].set(c[0]).at[1].set(c_ctx)
    mods = _mod_call(cv, w_mod, b_mod)[:, :2]
    cos_t, sin_t = _rope_tables(t, t_ctx, nh)
    perm = _swap_perm()
    row2 = lambda a: a.reshape(1, -1)

    for li in range(depth):
        last = li == depth - 1
        mod = mods[li]
        wi = w_in[li]
        o_kr = 2 * aw + 2 * bw + ql + kvl
        win = jnp.concatenate([wi, wi[:, o_kr:o_kr + C_ROPE][:, perm]], axis=1).astype(BF16)
        wq = w_uq[li].reshape(ql, nh, C_NOPE + C_ROPE)
        wq_r = wq[:, :, C_NOPE:]
        wuq = jnp.concatenate([wq[:, :, :C_NOPE].reshape(ql, -1), wq_r.reshape(ql, -1), wq_r[:, :, perm].reshape(ql, -1)],
                              axis=1).astype(BF16)
        wkv = w_ukv[li].reshape(kvl, nh, C_NOPE + C_V)
        wukv = jnp.concatenate([wkv[:, :, :C_NOPE].reshape(kvl, -1), wkv[:, :, C_NOPE:].reshape(kvl, -1)], axis=1).astype(BF16)
        sguw = sgu_w[li].transpose(1, 0, 2).reshape(CHUNK, A_GROUPS * CHUNK).astype(BF16)
        sgub = jnp.repeat(sgu_b[li].T, aw // A_GROUPS, axis=1)

        ya, xb, gb, q, kt, v = _in_call(xl, xc, mod, row2(norm1_g[li]), win, sguw, sgub, row2(q_norm_g[li]), wuq,
                                        row2(kv_norm_g[li]), wukv, cos_t, sin_t, nh=nh)

        lru = lambda dd: (conv_w[li], row2(conv_b[li]), jax.scipy.linalg.block_diag(*lru_wa[li, dd]).astype(BF16),
                          row2(lru_ba[li, dd]), jax.scipy.linalg.block_diag(*lru_wx[li, dd]).astype(BF16),
                          row2(lru_bx[li, dd]), row2(lru_lambda[li, dd]))
        hf = _lru_call(xb, None, None, *lru(0), reverse=False)
        yb = _lru_call(xb, hf, gb, *lru(1), reverse=True)

        ycl = _attn_call(q, kt, v, q_row0=0, nq_rows=t, kv_row0=0, kv_len=n, tq=Q_TILE, tk=KV_CHUNK)
        ycc = ycl[:t_ctx] if last else _attn_call(q, kt, v, q_row0=t, nq_rows=t_ctx, kv_row0=t, kv_len=t_ctx, tq=t_ctx, tk=t_ctx)

        wr_hi = w_router[li].astype(BF16)
        wr_split = jnp.concatenate([wr_hi, (w_router[li] - wr_hi.astype(F32)).astype(BF16)], axis=1)
        x1, h2, probs = _out_call(xl, xc, ya, yb, ycl, ycc, w_out[li].astype(BF16), mod, row2(norm2_g[li]), wr_split,
                                  n_tiles=nlt if last else nt)
        lp = {'w_gate': w_gate, 'w_up': w_up, 'w_down': w_down}
        probs_t = probs.T
        xl = _moe(h2, probs_t[:, :t], probs[:t], x1, mod, row2(final_norm_g) if last else None, lp, layer=li, tile0=0, mod_row=0)
        if not last:
            xc = _moe(h2, probs_t[:, t:], probs[t:], x1, mod, None, lp, layer=li, tile0=nlt, mod_row=1)
    return xl[None]
```

```python
import functools

import jax
import jax.numpy as jnp
import jax.scipy.linalg
from jax import lax
from jax.experimental import pallas as pl
from jax.experimental.pallas import tpu as pltpu

F32 = jnp.float32
BF16 = jnp.bfloat16
HIGHEST = lax.Precision.HIGHEST

GRID_W = 64
CHUNK = 128
EPS = 1e-6
A_GROUPS = 4
CONV_W = 4
LRU_C = 8.0
C_NOPE = 128
C_ROPE = 64
C_V = 128
ROPE_BASE = 10000.0
EC_CAPACITY = 2
LOG2E = 1.4426950408889634
SMALLEST_NORMAL = 1.1754943508222875e-38
BISECT_STEPS = 48

TOKEN_TILE = 256
LANES = 128
Q_TILE = 4096
Q_SUBTILE = 1024
KV_CHUNK = 1280
KV_UNROLL = 2
DISPATCH_TILES = 8
DISPATCH_NARROW = 64
COMBINE_TILES = 8
COMBINE_EXPERTS = 2
FFN_ROWS = 256
VMEM_LIMIT = 60 * 1024 * 1024


def _cparams(sem, vmem=None):
    return pltpu.CompilerParams(dimension_semantics=sem, vmem_limit_bytes=vmem)


def _rms(x):
    return x * lax.rsqrt(jnp.mean(x * x, axis=-1, keepdims=True) + EPS)


def _mod_kernel(cv_ref, w_ref, b_ref, o_ref):
    cv = cv_ref[...]
    s = cv * jax.nn.sigmoid(cv)
    o_ref[...] = jnp.dot(s, w_ref[...], precision=HIGHEST, preferred_element_type=F32) + b_ref[...]


def _mod_call(cv, w_mod, b_mod):
    depth, d, d6 = w_mod.shape
    tn = 1536
    return pl.pallas_call(
        _mod_kernel,
        grid=(depth, d6 // tn),
        in_specs=[
            pl.BlockSpec((8, d), lambda l, n: (0, 0)),
            pl.BlockSpec((None, d, tn), lambda l, n: (l, 0, n)),
            pl.BlockSpec((None, 1, tn), lambda l, n: (l, 0, n)),
        ],
        out_specs=pl.BlockSpec((None, 8, tn), lambda l, n: (l, 0, n)),
        out_shape=jax.ShapeDtypeStruct((depth, 8, d6), F32),
        compiler_params=_cparams(("arbitrary", "arbitrary")),
        name="modulation",
    )(cv, w_mod, b_mod.reshape(depth, 1, d6))


def _in_kernel(xl_ref, xc_ref, mod_ref, g1_ref, win_ref, sguw_ref, sgub_ref, qg_ref, wuq_ref, kvg_ref, wukv_ref, cos_ref, sin_ref,
               ya_ref, xb_ref, gb_ref, q_ref, kt_ref, v_ref, *, n_lat_tiles, qscale):
    tm, d = xl_ref.shape
    is_ctx = pl.program_id(0) >= n_lat_tiles
    mod = mod_ref[...]
    m = jnp.where(is_ctx, mod[1:2], mod[0:1])
    shift, scale = m[:, 0:d], m[:, d:2 * d]
    h = _rms(jnp.where(is_ctx, xc_ref[...], xl_ref[...])) * g1_ref[...]
    h = h * (1.0 + scale) + shift
    z = jnp.dot(h.astype(BF16), win_ref[...], preferred_element_type=F32)

    aw = sgub_ref.shape[1]
    za = jax.nn.gelu(z[:, 0:2 * aw])
    u, val = za[:, :aw], za[:, aw:]
    vb = _rms(val).astype(BF16)
    gw = aw // A_GROUPS
    lane = lax.broadcasted_iota(jnp.int32, (CHUNK, aw), 1)
    for ci in range(tm // CHUNK):
        vc = vb[ci * CHUNK:(ci + 1) * CHUNK]
        vexp = jnp.concatenate(
            [jnp.where((lane >= g * gw) & (lane < (g + 1) * gw), vc, jnp.zeros_like(vc)) for g in range(A_GROUPS)], axis=0)
        mixed = jnp.dot(sguw_ref[...], vexp, preferred_element_type=F32) + sgub_ref[...]
        ya_ref[ci * CHUNK:(ci + 1) * CHUNK, :] = (u[ci * CHUNK:(ci + 1) * CHUNK] * mixed).astype(ya_ref.dtype)

    o = 2 * aw
    bw = xb_ref.shape[1]
    xb_ref[...] = z[:, o:o + bw]
    gb_ref[...] = jax.nn.gelu(z[:, o + bw:o + 2 * bw])
    o += 2 * bw

    ql = qg_ref.shape[1]
    kvl = kvg_ref.shape[1]
    nh = q_ref.shape[0]
    cos_t, sin_t = cos_ref[...], sin_ref[...]
    cqn = _rms(z[:, o:o + ql]) * qg_ref[...]
    qf = jnp.dot(cqn.astype(BF16), wuq_ref[...], preferred_element_type=F32)
    nn, nr = nh * C_NOPE, nh * C_ROPE
    qr = qf[:, nn:nn + nr] * cos_t + qf[:, nn + nr:nn + 2 * nr] * sin_t
    o += ql
    ckvn = _rms(z[:, o:o + kvl]) * kvg_ref[...]
    kvf = jnp.dot(ckvn.astype(BF16), wukv_ref[...], preferred_element_type=F32)
    o += kvl
    kr = z[:, o:o + C_ROPE] * cos_t[:, :C_ROPE] + z[:, o + C_ROPE:o + 2 * C_ROPE] * sin_t[:, :C_ROPE]
    dk = C_NOPE + C_ROPE
    kr_pad = jnp.concatenate([kr, jnp.zeros((tm, 2 * LANES - dk), F32)], axis=1)
    for hh in range(nh):
        q_ref[hh, :, 0:C_NOPE] = (qf[:, hh * C_NOPE:(hh + 1) * C_NOPE] * qscale).astype(q_ref.dtype)
        q_ref[hh, :, C_NOPE:dk] = (qr[:, hh * C_ROPE:(hh + 1) * C_ROPE] * qscale).astype(q_ref.dtype)
        kh = jnp.concatenate([kvf[:, hh * C_NOPE:(hh + 1) * C_NOPE], kr_pad], axis=1)
        kt_ref[hh] = kh.T[:dk].astype(kt_ref.dtype)
        v_ref[hh] = kvf[:, nn + hh * C_V:nn + (hh + 1) * C_V].astype(v_ref.dtype)


def _in_call(xl, xc, mod, g1, win, sguw, sgub, qg, wuq, kvg, wukv, cos_t, sin_t, *, nh):
    d = xl.shape[1]
    tm = TOKEN_TILE
    n_lat_tiles = xl.shape[0] // tm
    n = xl.shape[0] + xc.shape[0]
    aw = sgub.shape[1]
    bw = aw
    dk = C_NOPE + C_ROPE
    full = lambda a: pl.BlockSpec(a.shape, lambda i: (0,) * a.ndim)
    row = lambda w: pl.BlockSpec((tm, w), lambda i: (i, 0))
    kern = functools.partial(_in_kernel, n_lat_tiles=n_lat_tiles, qscale=float(dk ** -0.5 * LOG2E))
    return pl.pallas_call(
        kern,
        grid=(n // tm,),
        in_specs=[pl.BlockSpec((tm, d), lambda i: (jnp.minimum(i, n_lat_tiles - 1), 0)), full(xc),
                  full(mod), full(g1), full(win), full(sguw), full(sgub), full(qg), full(wuq), full(kvg), full(wukv),
                  row(cos_t.shape[1]), row(sin_t.shape[1])],
        out_specs=[row(aw), row(bw), row(bw),
                   pl.BlockSpec((nh, tm, dk), lambda i: (0, i, 0)),
                   pl.BlockSpec((nh, dk, tm), lambda i: (0, 0, i)),
                   pl.BlockSpec((nh, tm, C_V), lambda i: (0, i, 0))],
        out_shape=[jax.ShapeDtypeStruct((n, aw), BF16), jax.ShapeDtypeStruct((n, bw), F32), jax.ShapeDtypeStruct((n, bw), F32),
                   jax.ShapeDtypeStruct((nh, n, dk), BF16), jax.ShapeDtypeStruct((nh, dk, n), BF16),
                   jax.ShapeDtypeStruct((nh, n, C_V), BF16)],
        compiler_params=_cparams(("arbitrary",), VMEM_LIMIT),
        name="input_proj",
    )(xl, xc, mod, g1, win, sguw, sgub, qg, wuq, kvg, wukv, cos_t, sin_t)


def _tile_scan(a, b, reverse):
    n = a.shape[0]
    rows = lax.broadcasted_iota(jnp.int32, a.shape, 0)
    k = 1
    while k < n:
        sh = n - k if reverse else k
        valid = (rows < n - k) if reverse else (rows >= k)
        a_s = pltpu.roll(a, sh, 0)
        b_s = pltpu.roll(b, sh, 0)
        b = jnp.where(valid, a * b_s, 0.0) + b
        a = jnp.where(valid, a * a_s, a)
        k *= 2
    return a, b


def _softplus(x):
    return jnp.maximum(x, 0.0) + jnp.log(1.0 + jnp.exp(-jnp.abs(x)))


def _lru_kernel(*refs, reverse, nt):
    if reverse:
        xp_ref, x_ref, xn_ref, hf_ref, gb_ref, cw_ref, cb_ref, wa_ref, ba_ref, wx_ref, bx_ref, lam_ref, o_ref, carry_ref = refs
    else:
        xp_ref, x_ref, xn_ref, cw_ref, cb_ref, wa_ref, ba_ref, wx_ref, bx_ref, lam_ref, o_ref, carry_ref = refs
    tm = x_ref.shape[0]
    s = pl.program_id(0)
    tile = _lru_tile(s, nt, reverse)

    @pl.when(s == 0)
    def _():
        carry_ref[...] = jnp.zeros_like(carry_ref)

    prev_ok = jnp.logical_and(tile != 0, tile != nt - 1)
    next_ok = tile < nt - 2
    xp = jnp.where(prev_ok, xp_ref[...], 0.0)
    xn = jnp.where(next_ok, xn_ref[...], 0.0)
    ext = jnp.concatenate([xp, x_ref[...], xn], axis=0)
    cw = cw_ref[...]
    conv = cb_ref[...]
    for kk in range(CONV_W):
        st = 8 + kk - CONV_W // 2
        conv = conv + ext[st:st + tm] * cw[kk:kk + 1]

    cb16 = conv.astype(BF16)
    r = jax.nn.sigmoid(jnp.dot(cb16, wa_ref[...], preferred_element_type=F32) + ba_ref[...])
    gate = jax.nn.sigmoid(jnp.dot(cb16, wx_ref[...], preferred_element_type=F32) + bx_ref[...])
    log_a = (-LRU_C) * r * _softplus(-lam_ref[...])
    a = jnp.exp(log_a)
    mult = jnp.sqrt(jnp.maximum(-jnp.tanh(log_a) * (a * a + 1.0), 0.0))
    b = mult * (gate * conv)
    acum, bcum = _tile_scan(a, b, reverse)
    h = acum * carry_ref[...] + bcum
    carry_ref[...] = h[0:1] if reverse else h[tm - 1:tm]
    if reverse:
        o_ref[...] = ((hf_ref[...] + h) * gb_ref[...]).astype(o_ref.dtype)
    else:
        o_ref[...] = h


def _lru_tile(s, nt, reverse):
    if reverse:
        return jnp.where(s == 0, nt - 1, nt - 1 - s)
    return jnp.where(s == 0, nt - 1, s - 1)


def _lru_call(xb, hf, gb, cw, cb, wa, ba, wx, bx, lam, *, reverse):
    n, w = xb.shape
    tm = TOKEN_TILE
    nt = n // tm
    r8 = tm // 8
    tile = lambda s: _lru_tile(s, nt, reverse)
    row = pl.BlockSpec((tm, w), lambda s: (tile(s), 0))
    prev = pl.BlockSpec((8, w), lambda s: (jnp.maximum(tile(s) * r8 - 1, 0), 0))
    nxt = pl.BlockSpec((8, w), lambda s: (jnp.minimum((tile(s) + 1) * r8, n // 8 - 1), 0))
    full = lambda a: pl.BlockSpec(a.shape, lambda s: (0,) * a.ndim)
    params = [cw, cb, wa, ba, wx, bx, lam]
    ins = [xb, xb, xb] + ([hf, gb] if reverse else []) + params
    specs = [prev, row, nxt] + ([row, row] if reverse else []) + [full(p) for p in params]
    return pl.pallas_call(
        functools.partial(_lru_kernel, reverse=reverse, nt=nt),
        grid=(nt,),
        in_specs=specs,
        out_specs=row,
        out_shape=jax.ShapeDtypeStruct((n, w), BF16 if reverse else F32),
        scratch_shapes=[pltpu.VMEM((1, w), F32)],
        compiler_params=_cparams(("arbitrary",)),
        name="lru_bwd" if reverse else "lru_fwd",
    )(*ins)


def _attn_kernel(*refs, tk, nk, ts):
    q_ref, kt_ref, v_ref, o_ref = refs[:4]
    scratch = refs[4:]
    nsub = q_ref.shape[0] // ts
    m_refs, l_refs, acc_refs = scratch[:nsub], scratch[nsub:2 * nsub], scratch[2 * nsub:]
    lanes = m_refs[0].shape[1]
    nblk = tk // lanes
    for u in range(nsub):
        m_refs[u][...] = jnp.full(m_refs[u].shape, -jnp.inf, F32)
        l_refs[u][...] = jnp.zeros(l_refs[u].shape, F32)
        acc_refs[u][...] = jnp.zeros(acc_refs[u].shape, F32)

    def body(c, carry):
        off = pl.multiple_of(c * tk, tk)
        kt = kt_ref[:, pl.ds(off, tk)]
        v = v_ref[pl.ds(off, tk), :]
        for u in range(nsub):
            q = q_ref[u * ts:(u + 1) * ts, :]
            s = jnp.dot(q, kt, preferred_element_type=F32)
            m_prev = m_refs[u][...]
            m_new = jnp.maximum(m_prev, jnp.max(s, axis=-1, keepdims=True))
            alpha = jnp.exp2(m_prev - m_new)
            pb = [jnp.exp2(s[:, b * lanes:(b + 1) * lanes] - m_new) for b in range(nblk)]
            psum = pb[0]
            for b in range(1, nblk):
                psum = psum + pb[b]
            l_refs[u][...] = alpha * l_refs[u][...] + psum
            p = jnp.concatenate(pb, axis=1).astype(v.dtype)
            acc_refs[u][...] = alpha * acc_refs[u][...] + jnp.dot(p, v, preferred_element_type=F32)
            m_refs[u][...] = m_new
        return carry

    lax.fori_loop(0, nk, body, 0, unroll=min(nk, KV_UNROLL))
    for u in range(nsub):
        l = jnp.sum(l_refs[u][...], axis=-1, keepdims=True)
        o_ref[u * ts:(u + 1) * ts, :] = (acc_refs[u][...] / l).astype(o_ref.dtype)


def _attn_call(q, kt, v, *, q_row0, nq_rows, kv_row0, kv_len, tq, tk):
    nh, _, dk = q.shape
    dv = v.shape[2]
    ts = min(tq, Q_SUBTILE)
    assert q_row0 % tq == 0 and nq_rows % tq == 0 and kv_row0 % kv_len == 0 and kv_len % tk == 0 and tq % ts == 0
    assert dv == LANES and tk % LANES == 0
    nsub = tq // ts
    qb0, kvb = q_row0 // tq, kv_row0 // kv_len
    in_specs = [pl.BlockSpec((None, tq, dk), lambda h, i: (h, qb0 + i, 0)),
                pl.BlockSpec((None, dk, kv_len), lambda h, i: (h, 0, kvb)),
                pl.BlockSpec((None, kv_len, dv), lambda h, i: (h, kvb, 0))]
    return pl.pallas_call(
        functools.partial(_attn_kernel, tk=tk, nk=kv_len // tk, ts=ts),
        grid=(nh, nq_rows // tq),
        in_specs=in_specs,
        out_specs=pl.BlockSpec((tq, dv), lambda h, i: (i, h)),
        out_shape=jax.ShapeDtypeStruct((nq_rows, nh * dv), BF16),
        scratch_shapes=[pltpu.VMEM((ts, LANES), F32)] * (3 * nsub),
        compiler_params=_cparams(("arbitrary", "arbitrary"), VMEM_LIMIT),
        name="attention",
    )(q, kt, v)


def _out_kernel(xl_ref, xc_ref, ya_ref, yb_ref, ycl_ref, ycc_ref, wout_ref, mod_ref, g2_ref, wr_ref, x1_ref, h2_ref, p_ref, *,
                n_lat_tiles):
    d = xl_ref.shape[1]
    is_ctx = pl.program_id(0) >= n_lat_tiles
    x = jnp.where(is_ctx, xc_ref[...], xl_ref[...])
    yc = jnp.where(is_ctx, ycc_ref[...], ycl_ref[...])
    mod = mod_ref[...]
    m = jnp.where(is_ctx, mod[1:2], mod[0:1])
    gate1, shift2, scale2 = m[:, 2 * d:3 * d], m[:, 3 * d:4 * d], m[:, 4 * d:5 * d]
    mix = jnp.concatenate([ya_ref[...], yb_ref[...], yc], axis=-1)
    x1 = x + gate1 * jnp.dot(mix, wout_ref[...], preferred_element_type=F32)
    x1_ref[...] = x1
    h2 = _rms(x1) * g2_ref[...]
    h2 = h2 * (1.0 + scale2) + shift2
    h2_ref[...] = h2.astype(h2_ref.dtype)
    ne = p_ref.shape[1]
    h_hi = h2.astype(BF16)
    h_lo = (h2 - h_hi.astype(F32)).astype(BF16)
    wr = wr_ref[...]
    part = jnp.dot(h_hi, wr, preferred_element_type=F32)
    logits = part[:, :ne] + part[:, ne:] + jnp.dot(h_lo, wr[:, :ne], preferred_element_type=F32)
    ex = jnp.exp(logits - jnp.max(logits, axis=-1, keepdims=True))
    p_ref[...] = ex / jnp.sum(ex, axis=-1, keepdims=True)


def _out_call(xl, xc, ya, yb, ycl, ycc, wout, mod, g2, wr, *, n_tiles):
    d = xl.shape[1]
    tm = TOKEN_TILE
    n_lat_tiles = xl.shape[0] // tm
    n = n_tiles * tm
    lat = lambda w: pl.BlockSpec((tm, w), lambda i: (jnp.minimum(i, n_lat_tiles - 1), 0))
    ne = wr.shape[1] // 2
    full = lambda a: pl.BlockSpec(a.shape, lambda i: (0,) * a.ndim)
    row = lambda w: pl.BlockSpec((tm, w), lambda i: (i, 0))
    return pl.pallas_call(
        functools.partial(_out_kernel, n_lat_tiles=n_lat_tiles),
        grid=(n_tiles,),
        in_specs=[lat(d), full(xc), row(ya.shape[1]), row(yb.shape[1]), lat(ycl.shape[1]), full(ycc),
                  full(wout), full(mod), full(g2), full(wr)],
        out_specs=[row(d), row(d), row(ne)],
        out_shape=[jax.ShapeDtypeStruct((n, d), F32), jax.ShapeDtypeStruct((n, d), BF16), jax.ShapeDtypeStruct((n, ne), F32)],
        compiler_params=_cparams(("arbitrary",), VMEM_LIMIT),
        name="output_proj",
    )(xl, xc, ya, yb, ycl, ycc, wout, mod, g2, wr)


def _cumsum_lanes(x):
    n = x.shape[1]
    lane = lax.broadcasted_iota(jnp.int32, x.shape, 1)
    k = 1
    while k < n:
        x = x + jnp.where(lane >= k, pltpu.roll(x, k, 1), 0)
        k *= 2
    return x


def _count(mask):
    return jnp.sum(jnp.where(mask, 1.0, 0.0), axis=1, keepdims=True).astype(jnp.int32)


def _select_kernel(p_ref, pos_ref, excl_ref, *, cap):
    p = p_ref[...]

    def step(i, bounds):
        lo, hi = bounds
        lo_pos = jnp.maximum(lo, SMALLEST_NORMAL)
        mid = jnp.where(hi > 2.0 * lo_pos, jnp.sqrt(lo_pos) * jnp.sqrt(hi), lo + 0.5 * (hi - lo))
        mid = jnp.clip(mid, lo, hi)
        ok = _count(p >= mid) >= cap
        return jnp.where(ok, mid, lo), jnp.where(ok, hi, mid)

    rows = (p.shape[0], 1)
    lo, hi = lax.fori_loop(0, BISECT_STEPS, step, (jnp.zeros(rows, F32), jnp.full(rows, 2.0, F32)))
    gt = p >= hi
    eq = jnp.logical_and(p >= lo, p < hi)
    need = cap - _count(gt)
    eqi = jnp.where(eq, 1, 0)
    eq_rank = _cumsum_lanes(eqi) - eqi
    sel = jnp.logical_or(gt, jnp.logical_and(eq, eq_rank < need))
    seli = jnp.where(sel, 1, 0)
    excl = _cumsum_lanes(seli) - seli
    excl_ref[...] = excl
    pos_ref[...] = jnp.where(sel, excl, -1)


def _select_call(p_t, cap):
    ne, t = p_t.shape
    return pl.pallas_call(
        functools.partial(_select_kernel, cap=cap),
        out_shape=[jax.ShapeDtypeStruct((ne, t), jnp.int32), jax.ShapeDtypeStruct((ne, t), jnp.int32)],
        compiler_params=pltpu.CompilerParams(vmem_limit_bytes=VMEM_LIMIT),
        name="expert_select",
    )(p_t)


def _moe_ffn_kernel(c0_ref, pos_ref, h_ref, wg_ref, wu_ref, wd_ref, y_ref, xs_ref, *, nb, nsub, win, win_small, cap, rows):
    e = pl.program_id(0)
    j = pl.program_id(1)
    tm = TOKEN_TILE

    @pl.when(j == 0)
    def _():
        xs_ref[...] = jnp.zeros_like(xs_ref)

    base = e * (nb + 1) + j * nsub
    starts = [pl.multiple_of((c0_ref[base + u] // 8) * 8, 8) for u in range(nsub)]

    def dispatch(rows_w):
        slot = lax.broadcasted_iota(jnp.int32, (rows_w, tm), 0)
        for u in range(nsub):
            rel = pos_ref[:, u * tm:(u + 1) * tm] - starts[u]
            onehot = jnp.where(slot == rel, 1.0, 0.0).astype(BF16)
            xs_ref[pl.ds(starts[u], rows_w), :] += jnp.dot(onehot, h_ref[u * tm:(u + 1) * tm, :], preferred_element_type=F32)

    if win_small < win:
        narrow = c0_ref[base + 1] - starts[0] <= win_small
        for u in range(1, nsub):
            narrow = jnp.logical_and(narrow, c0_ref[base + u + 1] - starts[u] <= win_small)
        pl.when(narrow)(lambda: dispatch(win_small))
        pl.when(jnp.logical_not(narrow))(lambda: dispatch(win))
    else:
        dispatch(win)

    @pl.when(j == pl.num_programs(1) - 1)
    def _():
        wg = wg_ref[...].astype(BF16)
        wu = wu_ref[...].astype(BF16)
        wd = wd_ref[...].astype(BF16)
        for ci in range(cap // rows):
            xc = xs_ref[ci * rows:(ci + 1) * rows, :].astype(BF16)
            hg = jnp.dot(xc, wg, preferred_element_type=F32)
            hu = jnp.dot(xc, wu, preferred_element_type=F32)
            hid = (hg * jax.nn.sigmoid(hg) * hu).astype(BF16)
            y_ref[ci * rows:(ci + 1) * rows, :] = jnp.dot(hid, wd, preferred_element_type=F32).astype(y_ref.dtype)


def _moe_ffn_call(c0, pos_t, h2, wg, wu, wd, *, layer, cap, tile0):
    ne, t = pos_t.shape
    tm = TOKEN_TILE
    nb = t // tm
    nsub = min(nb, DISPATCH_TILES)
    gt = nsub * tm
    assert nb % nsub == 0 and tile0 % nsub == 0
    d = h2.shape[1]
    ff = wg.shape[3]
    win = min(tm, cap) + 16
    win_small = min(win, DISPATCH_NARROW)
    rows = min(cap, FFN_ROWS)
    return pl.pallas_call(
        functools.partial(_moe_ffn_kernel, nb=nb, nsub=nsub, win=win, win_small=win_small, cap=cap, rows=rows),
        grid_spec=pltpu.PrefetchScalarGridSpec(
            num_scalar_prefetch=1,
            grid=(ne, nb // nsub),
            in_specs=[pl.BlockSpec((None, 1, gt), lambda e, j, c: (e, 0, j)),
                      pl.BlockSpec((gt, d), lambda e, j, c: (tile0 // nsub + j, 0)),
                      pl.BlockSpec((None, None, d, ff), lambda e, j, c: (layer, e, 0, 0)),
                      pl.BlockSpec((None, None, d, ff), lambda e, j, c: (layer, e, 0, 0)),
                      pl.BlockSpec((None, None, ff, d), lambda e, j, c: (layer, e, 0, 0))],
            out_specs=pl.BlockSpec((None, cap, d), lambda e, j, c: (e, 0, 0)),
            scratch_shapes=[pltpu.VMEM((cap + win, d), F32)]),
        out_shape=jax.ShapeDtypeStruct((ne, cap, d), BF16),
        compiler_params=_cparams(("arbitrary", "arbitrary"), VMEM_LIMIT),
        name="moe_ffn",
    )(c0, pos_t.reshape(ne, 1, t), h2, wg, wu, wd)


def _combine_kernel(*refs, nb, nsub, nes, kwin, kwin_small, cap, mod_row, final):
    c0_ref, y_refs, refs = refs[0], refs[1:1 + nes], refs[1 + nes:]
    if final:
        pos_ref, p_ref, x1_ref, mod_ref, fg_ref, o_ref = refs
    else:
        pos_ref, p_ref, x1_ref, mod_ref, o_ref = refs
    g = pl.program_id(0)
    es = pl.program_id(1)
    tm = TOKEN_TILE
    d = x1_ref.shape[1]
    gate2 = mod_ref[mod_row:mod_row + 1, 5 * d:6 * d]

    @pl.when(es == 0)
    def _():
        o_ref[...] = x1_ref[...]

    experts = [es * nes + i for i in range(nes)]
    lane = lax.broadcasted_iota(jnp.int32, (tm, pos_ref.shape[1]), 1)
    bases = [e * (nb + 1) + g * nsub for e in experts]

    def combine(kw):
        slot = lax.broadcasted_iota(jnp.int32, (tm, kw), 1)
        for u in range(nsub):
            rs = slice(u * tm, (u + 1) * tm)
            upd = None
            for e, base, y_ref in zip(experts, bases, y_refs):
                col = lane == e
                w0 = pl.multiple_of(jnp.minimum((c0_ref[base + u] // 16) * 16, cap - kw), 16)
                slot_of = jnp.sum(jnp.where(col, pos_ref[rs, :].astype(F32), 0.0), axis=1, keepdims=True).astype(jnp.int32)
                gate = jnp.sum(jnp.where(col, p_ref[rs, :], 0.0), axis=1, keepdims=True)
                onehot = jnp.where(slot == slot_of - w0, 1.0, 0.0).astype(BF16)
                r = gate * (gate2 * jnp.dot(onehot, y_ref[pl.ds(w0, kw), :], preferred_element_type=F32))
                upd = r if upd is None else upd + r
            o_ref[rs, :] += upd

    if kwin_small < kwin:
        narrow = None
        for base in bases:
            for u in range(nsub):
                w0s = jnp.minimum((c0_ref[base + u] // 16) * 16, cap - kwin_small)
                fits = c0_ref[base + u + 1] - w0s <= kwin_small
                narrow = fits if narrow is None else jnp.logical_and(narrow, fits)
        pl.when(narrow)(lambda: combine(kwin_small))
        pl.when(jnp.logical_not(narrow))(lambda: combine(kwin))
    else:
        combine(kwin)

    if final:
        @pl.when(es == pl.num_programs(1) - 1)
        def _():
            o_ref[...] = _rms(o_ref[...]) * fg_ref[...]


def _combine_call(c0, y, pos, probs, x1, mod, fg, *, cap, tile0, mod_row):
    ne = y.shape[0]
    t = pos.shape[0]
    tm = TOKEN_TILE
    nb = t // tm
    nsub = min(nb, COMBINE_TILES)
    gt = nsub * tm
    assert nb % nsub == 0 and tile0 % nsub == 0
    d = x1.shape[1]
    kwin = min(2 * tm, cap)
    kwin_small = min(tm, kwin)
    nes = COMBINE_EXPERTS
    assert ne % nes == 0
    final = fg is not None
    full = lambda a: pl.BlockSpec(a.shape, lambda g, e, c: (0,) * a.ndim)
    in_specs = [pl.BlockSpec((None, cap, d), lambda g, e, c, i=i: (e * nes + i, 0, 0)) for i in range(nes)]
    in_specs += [pl.BlockSpec((gt, ne), lambda g, e, c: (g, 0)),
                pl.BlockSpec((gt, ne), lambda g, e, c: (g, 0)),
                pl.BlockSpec((gt, d), lambda g, e, c: (tile0 // nsub + g, 0)),
                full(mod)]
    ins = [y] * nes + [pos, probs, x1, mod]
    if final:
        in_specs.append(full(fg))
        ins.append(fg)
    return pl.pallas_call(
        functools.partial(_combine_kernel, nb=nb, nsub=nsub, nes=nes, kwin=kwin, kwin_small=kwin_small, cap=cap,
                          mod_row=mod_row, final=final),
        grid_spec=pltpu.PrefetchScalarGridSpec(
            num_scalar_prefetch=1,
            grid=(nb // nsub, ne // nes),
            in_specs=in_specs,
            out_specs=pl.BlockSpec((gt, d), lambda g, e, c: (g, 0))),
        out_shape=jax.ShapeDtypeStruct((t, d), F32),
        compiler_params=_cparams(("arbitrary", "arbitrary"), VMEM_LIMIT),
        name="moe_combine",
    )(c0, *ins)


def _rope_tables(t, t_ctx, nh):
    pos = jnp.arange(t)
    r = (pos // GRID_W).astype(F32)
    col = (pos % GRID_W).astype(F32)
    nf = C_ROPE // 4
    inv = ROPE_BASE ** (-jnp.arange(nf, dtype=F32) / nf)
    ar, ac = r[:, None] * inv, col[:, None] * inv
    cos_t = jnp.concatenate([jnp.cos(ar), jnp.cos(ar), jnp.cos(ac), jnp.cos(ac)], axis=-1)
    sin_t = jnp.concatenate([-jnp.sin(ar), jnp.sin(ar), -jnp.sin(ac), jnp.sin(ac)], axis=-1)
    cos_t = jnp.concatenate([cos_t, jnp.ones((t_ctx, C_ROPE), F32)], axis=0)
    sin_t = jnp.concatenate([sin_t, jnp.zeros((t_ctx, C_ROPE), F32)], axis=0)
    return jnp.tile(cos_t, (1, nh)), jnp.tile(sin_t, (1, nh))


def _swap_perm():
    nf = C_ROPE // 4
    idx = jnp.arange(C_ROPE).reshape(2, 2, nf)
    return idx[:, ::-1, :].reshape(-1)


def _moe(h2, probs_t, probs, x1, mod, fg, lp, *, layer, tile0, mod_row):
    ne, t = probs_t.shape
    cap = EC_CAPACITY * t // ne
    pos_t, excl = _select_call(probs_t, cap)
    c0 = jnp.concatenate([excl[:, ::TOKEN_TILE], jnp.full((ne, 1), cap, jnp.int32)], axis=1).reshape(-1)
    y = _moe_ffn_call(c0, pos_t, h2, lp['w_gate'], lp['w_up'], lp['w_down'], layer=layer, cap=cap, tile0=tile0)
    return _combine_call(c0, y, pos_t.T, probs, x1, mod, fg, cap=cap, tile0=tile0, mod_row=mod_row)


def kernel(x, c, ctx, c_ctx, norm1_g, w_mod, b_mod, w_in, sgu_w, sgu_b, conv_w, conv_b, lru_wa, lru_ba, lru_wx, lru_bx, lru_lambda, q_norm_g, w_uq, kv_norm_g, w_ukv, w_out, norm2_g, w_router, w_gate, w_up, w_down, final_norm_g):
    bsz, t, d = x.shape
    t_ctx = ctx.shape[1]
    depth = w_mod.shape[0]
    tm = TOKEN_TILE
    assert bsz == 1 and t_ctx == tm and t % Q_TILE == 0 and t % GRID_W == 0
    n = t + t_ctx
    nlt = t // tm
    nt = n // tm
    ql, kvl = q_norm_g.shape[1], kv_norm_g.shape[1]
    nh = w_uq.shape[2] // (C_NOPE + C_ROPE)
    aw = bw = (w_in.shape[2] - ql - kvl - C_ROPE) // 4
    assert (n % KV_CHUNK) == 0

    xl, xc = x[0], ctx[0]
    cv = jnp.zeros((8, d), F32).at[0].set(c[0]).at[1].set(c_ctx)
    mods = _mod_call(cv, w_mod, b_mod)[:, :2]
    cos_t, sin_t = _rope_tables(t, t_ctx, nh)
    perm = _swap_perm()
    row2 = lambda a: a.reshape(1, -1)

    for li in range(depth):
        last = li == depth - 1
        mod = mods[li]
        wi = w_in[li]
        o_kr = 2 * aw + 2 * bw + ql + kvl
        win = jnp.concatenate([wi, wi[:, o_kr:o_kr + C_ROPE][:, perm]], axis=1).astype(BF16)
        wq = w_uq[li].reshape(ql, nh, C_NOPE + C_ROPE)
        wq_r = wq[:, :, C_NOPE:]
        wuq = jnp.concatenate([wq[:, :, :C_NOPE].reshape(ql, -1), wq_r.reshape(ql, -1), wq_r[:, :, perm].reshape(ql, -1)],
                              axis=1).astype(BF16)
        wkv = w_ukv[li].reshape(kvl, nh, C_NOPE + C_V)
        wukv = jnp.concatenate([wkv[:, :, :C_NOPE].reshape(kvl, -1), wkv[:, :, C_NOPE:].reshape(kvl, -1)], axis=1).astype(BF16)
        sguw = sgu_w[li].transpose(1, 0, 2).reshape(CHUNK, A_GROUPS * CHUNK).astype(BF16)
        sgub = jnp.repeat(sgu_b[li].T, aw // A_GROUPS, axis=1)

        ya, xb, gb, q, kt, v = _in_call(xl, xc, mod, row2(norm1_g[li]), win, sguw, sgub, row2(q_norm_g[li]), wuq,
                                        row2(kv_norm_g[li]), wukv, cos_t, sin_t, nh=nh)

        lru = lambda dd: (conv_w[li], row2(conv_b[li]), jax.scipy.linalg.block_diag(*lru_wa[li, dd]).astype(BF16),
                          row2(lru_ba[li, dd]), jax.scipy.linalg.block_diag(*lru_wx[li, dd]).astype(BF16),
                          row2(lru_bx[li, dd]), row2(lru_lambda[li, dd]))
        hf = _lru_call(xb, None, None, *lru(0), reverse=False)
        yb = _lru_call(xb, hf, gb, *lru(1), reverse=True)

        ycl = _attn_call(q, kt, v, q_row0=0, nq_rows=t, kv_row0=0, kv_len=n, tq=Q_TILE, tk=KV_CHUNK)
        ycc = ycl[:t_ctx] if last else _attn_call(q, kt, v, q_row0=t, nq_rows=t_ctx, kv_row0=t, kv_len=t_ctx, tq=t_ctx, tk=t_ctx)

        wr_hi = w_router[li].astype(BF16)
        wr_split = jnp.concatenate([wr_hi, (w_router[li] - wr_hi.astype(F32)).astype(BF16)], axis=1)
        x1, h2, probs = _out_call(xl, xc, ya, yb, ycl, ycc, w_out[li].astype(BF16), mod, row2(norm2_g[li]), wr_split,
                                  n_tiles=nlt if last else nt)
        lp = {'w_gate': w_gate, 'w_up': w_up, 'w_down': w_down}
        probs_t = probs.T
        xl = _moe(h2, probs_t[:, :t], probs[:t], x1, mod, row2(final_norm_g) if last else None, lp, layer=li, tile0=0, mod_row=0)
        if not last:
            xc = _moe(h2, probs_t[:, t:], probs[t:], x1, mod, None, lp, layer=li, tile0=nlt, mod_row=1)
    return xl[None]
```
